```python
import math
import jax, jax.numpy as jnp
from jax import lax
import numpy as np

D_MODEL = 1024
BATCH = 2
SEQ = 8192
DEPTH = 2

GRID_W = 64
CTX_LEN = 256
D_MIX = D_MODEL
HEAD_DIM = 64
A_WIDTH = D_MIX // 2
A_HEADS = A_WIDTH // HEAD_DIM
A_KV_HEADS = A_HEADS // 4
A_GROUP = A_HEADS // A_KV_HEADS
B_WIDTH = D_MIX // 4
B_GROUPS = B_WIDTH // HEAD_DIM
CHUNK = 128
C_WIDTH = D_MIX // 4
C_HEADS = C_WIDTH // HEAD_DIM
C_QK_DIM = HEAD_DIM // 2
D_FF = 256 * ((8 * D_MODEL // 3 + 255) // 256)
CONV_W = 3
Q_BLOCK = 128
ROPE_THETA = 10000.0
EPS = 1e-6
N_MOD = 6

IN_SIZES = (A_WIDTH, A_KV_HEADS * HEAD_DIM, A_KV_HEADS * HEAD_DIM,
            B_WIDTH, B_WIDTH,
            C_WIDTH, C_WIDTH, C_WIDTH)
IN_SPLITS = tuple(int(s) for s in np.cumsum(IN_SIZES)[:-1])
D_IN = int(sum(IN_SIZES))

kernel_name = "hybrid_headgroup_dit_block"


def rms_norm(x, g):
    xf = x.astype(jnp.float32)
    y = xf * lax.rsqrt(jnp.mean(xf * xf, axis=-1, keepdims=True) + EPS)
    return (y * g.astype(jnp.float32)).astype(x.dtype)


def modulate(h, shift, scale):
    return h * (1 + scale) + shift


def axial_rope(rows, dim):
    row = jnp.repeat(jnp.arange(rows, dtype=jnp.float32), GRID_W)
    col = jnp.tile(jnp.arange(GRID_W, dtype=jnp.float32), rows)
    n_ax = dim // 4
    inv = ROPE_THETA ** (-jnp.arange(n_ax, dtype=jnp.float32) / n_ax)
    ang = jnp.concatenate([row[:, None] * inv, col[:, None] * inv], axis=-1)
    return jnp.cos(ang), jnp.sin(ang)


def apply_rope(x, cos, sin):
    shp = (1, cos.shape[0]) + (1,) * (x.ndim - 3) + (cos.shape[-1],)
    c, s = cos.reshape(shp), sin.reshape(shp)
    xf = x.astype(jnp.float32)
    h = x.shape[-1] // 2
    x1, x2 = xf[..., :h], xf[..., h:]
    return jnp.concatenate([x1 * c - x2 * s, x1 * s + x2 * c], axis=-1).astype(x.dtype)


def gqa_attend(q, k, v):
    s = jnp.einsum('bqhgd,bshd->bhgqs', q, k).astype(jnp.float32) * (q.shape[-1] ** -0.5)
    p = jax.nn.softmax(s, axis=-1).astype(v.dtype)
    return jnp.einsum('bhgqs,bshd->bqhgd', p, v)


def diff_attend(q, k, v, lam):
    scale = q.shape[-1] ** -0.5
    s1 = jnp.einsum('bqhd,bshd->bhqs', q[..., 0, :], k[..., 0, :]).astype(jnp.float32) * scale
    s2 = jnp.einsum('bqhd,bshd->bhqs', q[..., 1, :], k[..., 1, :]).astype(jnp.float32) * scale
    a = (jax.nn.softmax(s1, axis=-1) - lam * jax.nn.softmax(s2, axis=-1)).astype(v.dtype)
    return jnp.einsum('bhqs,bshd->bqhd', a, v)


def sweep_blocks(fn, q):
    b, n = q.shape[:2]
    nb = n // Q_BLOCK
    qb = jnp.moveaxis(q.reshape((b, nb, Q_BLOCK) + q.shape[2:]), 1, 0)
    out = jnp.moveaxis(lax.map(fn, qb), 0, 1)
    return out.reshape((b, n) + out.shape[3:])


def chunk_spatial_gate(u, v, w_s, b_s):
    bsz, t, _ = v.shape
    vc = v.reshape(bsz, t // CHUNK, CHUNK, B_GROUPS, HEAD_DIM)
    mixed = jnp.einsum('gpq,bcqgd->bcpgd', w_s, vc) + b_s.T[None, None, :, :, None]
    return u * mixed.reshape(bsz, t, B_WIDTH)


def dwconv(h, w, b):
    ch = h.shape[-1]
    y = lax.conv_general_dilated(h, w[:, None, :], window_strides=(1,),
                                 padding=((CONV_W // 2, CONV_W // 2),),
                                 dimension_numbers=('NWC', 'WIO', 'NWC'),
                                 feature_group_count=ch)
    return y + b


def conv_ffn(h, w_up, w_conv, b_conv, w_down):
    z = dwconv(h @ w_up, w_conv, b_conv)
    g, val = jnp.split(z, 2, axis=-1)
    return (jax.nn.silu(g) * val) @ w_down


def project(h, w_in, gq_a, gk_a, rope):
    b, t, _ = h.shape
    qa, ka, va, ub, vb, qc, kc, vc = jnp.split(h @ w_in, IN_SPLITS, axis=-1)
    qa = rms_norm(qa.reshape(b, t, A_HEADS, HEAD_DIM), gq_a)
    ka = rms_norm(ka.reshape(b, t, A_KV_HEADS, HEAD_DIM), gk_a)
    va = va.reshape(b, t, A_KV_HEADS, HEAD_DIM)
    qc = qc.reshape(b, t, C_HEADS, 2, C_QK_DIM)
    kc = kc.reshape(b, t, C_HEADS, 2, C_QK_DIM)
    vc = vc.reshape(b, t, C_HEADS, HEAD_DIM)
    if rope is not None:
        cos_a, sin_a, cos_c, sin_c = rope
        qa, ka = apply_rope(qa, cos_a, sin_a), apply_rope(ka, cos_a, sin_a)
        qc, kc = apply_rope(qc, cos_c, sin_c), apply_rope(kc, cos_c, sin_c)
    return qa, ka, va, ub, vb, qc, kc, vc


def token_mix(p, k_a, v_a, k_c, v_c, lam, lam_init, gv_b, w_s, b_s, g_sub_c, w_out, blocked):
    qa, _, _, ub, vb, qc, _, _ = p
    b, t = qa.shape[:2]
    qa = qa.reshape(b, t, A_KV_HEADS, A_GROUP, HEAD_DIM)
    fa = lambda q: gqa_attend(q, k_a, v_a)
    fc = lambda q: diff_attend(q, k_c, v_c, lam)
    if blocked:
        oa, oc = sweep_blocks(fa, qa), sweep_blocks(fc, qc)
    else:
        oa, oc = fa(qa), fc(qc)
    ob = chunk_spatial_gate(jax.nn.gelu(ub), rms_norm(jax.nn.gelu(vb), gv_b), w_s, b_s)
    oc = rms_norm(oc, g_sub_c) * (1.0 - lam_init)
    y = jnp.concatenate([oa.reshape(b, t, A_WIDTH), ob, oc.reshape(b, t, C_WIDTH)], axis=-1)
    return y @ w_out


def setup_inputs(seed: int = 0) -> dict:
    key = jax.random.key(seed)
    ks = jax.random.split(key, 26)
    nrm = lambda k, shp: jax.random.normal(k, shp, jnp.float32)
    L, D = DEPTH, D_MODEL
    return {
        "x": nrm(ks[0], (BATCH, SEQ, D)),
        "c": nrm(ks[1], (BATCH, D)),
        "ctx": nrm(ks[2], (BATCH, CTX_LEN, D)),
        "c_ctx": nrm(ks[3], (D,)),
        "w_mod": nrm(ks[4], (L, D, N_MOD * D)) * (0.5 * D ** -0.5),
        "b_mod": nrm(ks[5], (L, N_MOD * D)) * 0.01,
        "g_pre_mix": 1.0 + 0.02 * nrm(ks[6], (L, D)),
        "g_post_mix": 1.0 + 0.02 * nrm(ks[7], (L, D)),
        "w_in": nrm(ks[8], (L, D, D_IN)) * D ** -0.5,
        "gq_a": 1.0 + 0.02 * nrm(ks[9], (L, HEAD_DIM)),
        "gk_a": 1.0 + 0.02 * nrm(ks[10], (L, HEAD_DIM)),
        "gv_b": 1.0 + 0.02 * nrm(ks[11], (L, B_WIDTH)),
        "w_s": nrm(ks[12], (L, B_GROUPS, CHUNK, CHUNK)) * CHUNK ** -0.5,
        "b_s": nrm(ks[13], (L, B_GROUPS, CHUNK)) * 0.02,
        "lam_q1": nrm(ks[14], (L, C_QK_DIM)) * 0.1,
        "lam_k1": nrm(ks[15], (L, C_QK_DIM)) * 0.1,
        "lam_q2": nrm(ks[16], (L, C_QK_DIM)) * 0.1,
        "lam_k2": nrm(ks[17], (L, C_QK_DIM)) * 0.1,
        "g_sub_c": 1.0 + 0.02 * nrm(ks[18], (L, HEAD_DIM)),
        "w_out": nrm(ks[19], (L, D_MIX, D)) * D_MIX ** -0.5,
        "g_pre_ffn": 1.0 + 0.02 * nrm(ks[20], (L, D)),
        "g_post_ffn": 1.0 + 0.02 * nrm(ks[21], (L, D)),
        "w_up": nrm(ks[22], (L, D, 2 * D_FF)) * D ** -0.5,
        "w_conv": nrm(ks[23], (L, CONV_W, 2 * D_FF)) * CONV_W ** -0.5,
        "b_conv": nrm(ks[24], (L, 2 * D_FF)) * 0.02,
        "w_down": nrm(ks[25], (L, D_FF, D)) * D_FF ** -0.5,
    }


def reference(x, c, ctx, c_ctx, w_mod, b_mod, g_pre_mix, g_post_mix, w_in, gq_a, gk_a, gv_b,
              w_s, b_s, lam_q1, lam_k1, lam_q2, lam_k2, g_sub_c, w_out, g_pre_ffn, g_post_ffn,
              w_up, w_conv, b_conv, w_down):
    n = x.shape[1]
    rows = n // GRID_W
    rope = axial_rope(rows, HEAD_DIM) + axial_rope(rows, C_QK_DIM)
    xc = ctx
    sc, scc = jax.nn.silu(c), jax.nn.silu(c_ctx)
    for l in range(DEPTH):
        last = l == DEPTH - 1
        lam_init = 0.8 - 0.6 * math.exp(-0.3 * l)
        mx = [m[:, None, :] for m in jnp.split(sc @ w_mod[l] + b_mod[l], N_MOD, axis=-1)]
        mc = jnp.split(scc @ w_mod[l] + b_mod[l], N_MOD, axis=-1)
        lam = (jnp.exp(jnp.sum(lam_q1[l] * lam_k1[l]).astype(jnp.float32))
               - jnp.exp(jnp.sum(lam_q2[l] * lam_k2[l]).astype(jnp.float32)) + lam_init)

        px = project(modulate(rms_norm(x, g_pre_mix[l]), mx[0], mx[1]), w_in[l], gq_a[l], gk_a[l], rope)
        pc = project(modulate(rms_norm(xc, g_pre_mix[l]), mc[0], mc[1]), w_in[l], gq_a[l], gk_a[l], None)
        _, ka_c, va_c, _, _, _, kc_c, vc_c = pc
        _, ka_x, va_x, _, _, _, kc_x, vc_x = px
        k_a = jnp.concatenate([ka_c, ka_x], axis=1)
        v_a = jnp.concatenate([va_c, va_x], axis=1)
        k_c = jnp.concatenate([kc_c, kc_x], axis=1)
        v_c = jnp.concatenate([vc_c, vc_x], axis=1)
        mix_x = token_mix(px, k_a, v_a, k_c, v_c, lam, lam_init, gv_b[l], w_s[l], b_s[l],
                          g_sub_c[l], w_out[l], True)
        x = x + mx[2] * rms_norm(mix_x, g_post_mix[l])
        if not last:
            mix_c = token_mix(pc, ka_c, va_c, kc_c, vc_c, lam, lam_init, gv_b[l], w_s[l], b_s[l],
                              g_sub_c[l], w_out[l], False)
            xc = xc + mc[2] * rms_norm(mix_c, g_post_mix[l])

        fx = conv_ffn(modulate(rms_norm(x, g_pre_ffn[l]), mx[3], mx[4]), w_up[l], w_conv[l], b_conv[l], w_down[l])
        x = x + mx[5] * rms_norm(fx, g_post_ffn[l])
        if not last:
            fc = conv_ffn(modulate(rms_norm(xc, g_pre_ffn[l]), mc[3], mc[4]), w_up[l], w_conv[l], b_conv[l], w_down[l])
            xc = xc + mc[5] * rms_norm(fc, g_post_ffn[l])
    return x
```

```python
import functools
import math

import jax
import jax.numpy as jnp
from jax import lax
from jax.experimental import pallas as pl
from jax.experimental.pallas import tpu as pltpu

F32 = jnp.float32
BF16 = jnp.bfloat16

GRID_W = 64
HEAD_DIM = 64
A_HEADS = 8
A_KV_HEADS = 2
A_GROUP = A_HEADS // A_KV_HEADS
A_WIDTH = A_HEADS * HEAD_DIM
A_KV_WIDTH = A_KV_HEADS * HEAD_DIM
B_WIDTH = 256
B_GROUPS = B_WIDTH // HEAD_DIM
CHUNK = 128
C_HEADS = 4
C_WIDTH = C_HEADS * HEAD_DIM
C_QK_DIM = HEAD_DIM // 2
CONV_W = 3
ROPE_THETA = 10000.0
EPS = 1e-6
N_MOD = 6
LOG2E = 1.4426950408889634

LANES = 128
SUBLANES = 8
BF16_ROWS = 16
VMEM_LIMIT = 56 * 1024 * 1024
Q_TILE = 128
FF_CHUNK = 256
HALO = SUBLANES

_OFF_QA = 0
_OFF_KA = _OFF_QA + A_WIDTH
_OFF_VA = _OFF_KA + A_KV_WIDTH
_OFF_UB = _OFF_VA + A_KV_WIDTH
_OFF_VB = _OFF_UB + B_WIDTH
_OFF_QC = _OFF_VB + B_WIDTH
_OFF_KC = _OFF_QC + C_WIDTH
_OFF_VC = _OFF_KC + C_WIDTH
D_IN = _OFF_VC + C_WIDTH


def _cparams(sem):
    return pltpu.CompilerParams(dimension_semantics=sem, vmem_limit_bytes=VMEM_LIMIT)


def _rms(v):
    return v * lax.rsqrt(jnp.mean(v * v, axis=-1, keepdims=True) + EPS)


def _mod_kernel(c_ref, w_ref, b_ref, o_ref):
    s = jax.nn.silu(c_ref[...])
    o_ref[0, 0] = jnp.dot(s, w_ref[0], preferred_element_type=F32,
                          precision=lax.Precision.HIGHEST) + b_ref[0, 0]


def _modulation(cond, w_mod, b_mod):
    n_layers, d, _ = w_mod.shape
    rows = cond.shape[0]
    b3 = b_mod.reshape(n_layers, N_MOD, 1, d)
    return pl.pallas_call(
        _mod_kernel,
        out_shape=jax.ShapeDtypeStruct((n_layers, N_MOD, rows, d), F32),
        grid=(n_layers, N_MOD),
        in_specs=[
            pl.BlockSpec((rows, d), lambda l, m: (0, 0)),
            pl.BlockSpec((1, d, d), lambda l, m: (l, 0, m)),
            pl.BlockSpec((1, 1, 1, d), lambda l, m: (l, m, 0, 0)),
        ],
        out_specs=pl.BlockSpec((1, 1, rows, d), lambda l, m: (l, m, 0, 0)),
        compiler_params=_cparams(("arbitrary", "arbitrary")),
        name="modulation",
    )(cond, w_mod, b3)


def _swap_halves(v, half):
    lane = lax.broadcasted_iota(jnp.int32, v.shape, 1)
    first = (lane % (2 * half)) < half
    return jnp.where(first, pltpu.roll(v, LANES - half, 1), pltpu.roll(v, half, 1))


def _proj_kernel(x_ref, shift_ref, scale_ref, gpre_ref, win_ref, gqk_ref, bd_ref,
                 cosa_ref, sina_ref, cosc_ref, sinc_ref, gvb_ref, ws_ref, bsf_ref,
                 qat_ref, ka_ref, vat_ref, ob_ref, qct_ref, kc_ref, vct_ref):
    tm = x_ref.shape[1]
    h = _rms(x_ref[0]) * gpre_ref[...]
    h = h * (1.0 + scale_ref[0]) + shift_ref[0]
    p = jnp.dot(h.astype(BF16), win_ref[...], preferred_element_type=F32)

    cos_a, sin_a = cosa_ref[...], sina_ref[...]
    bd = bd_ref[...]
    a_blocks = []
    for j in range((A_WIDTH + A_KV_WIDTH) // LANES):
        blk = p[:, j * LANES:(j + 1) * LANES]
        sq = blk * blk
        hi = sq.astype(BF16)
        lo = (sq - hi.astype(F32)).astype(BF16)
        msq = (jnp.dot(hi, bd, preferred_element_type=F32)
               + jnp.dot(lo, bd, preferred_element_type=F32))
        nb = blk * lax.rsqrt(msq + EPS) * gqk_ref[:, j * LANES:(j + 1) * LANES]
        a_blocks.append(nb * cos_a + _swap_halves(nb, HEAD_DIM // 2) * sin_a)
    qa = jnp.concatenate(a_blocks[:A_WIDTH // LANES], axis=1) * (HEAD_DIM ** -0.5 * LOG2E)
    qat_ref[0] = qa.T.astype(BF16)
    ka_ref[0] = a_blocks[A_WIDTH // LANES].astype(BF16)
    vat_ref[0, 0] = p[:, _OFF_VA:_OFF_VA + A_KV_WIDTH].T.astype(BF16)

    cos_c, sin_c = cosc_ref[...], sinc_ref[...]

    def rope_c(off):
        blocks = []
        for j in range(C_WIDTH // LANES):
            blk = p[:, off + j * LANES:off + (j + 1) * LANES]
            blocks.append(blk * cos_c + _swap_halves(blk, C_QK_DIM // 2) * sin_c)
        return jnp.concatenate(blocks, axis=1)

    qct_ref[0] = (rope_c(_OFF_QC) * (C_QK_DIM ** -0.5 * LOG2E)).T.astype(BF16)
    kc_ref[0] = rope_c(_OFF_KC).astype(BF16)
    vct_ref[0, 0] = p[:, _OFF_VC:_OFF_VC + C_WIDTH].T.astype(BF16)

    u = jax.nn.gelu(p[:, _OFF_UB:_OFF_UB + B_WIDTH])
    vn = (_rms(jax.nn.gelu(p[:, _OFF_VB:_OFF_VB + B_WIDTH])) * gvb_ref[...]).astype(BF16)
    gid = lax.broadcasted_iota(jnp.int32, (CHUNK, B_WIDTH), 1) // HEAD_DIM
    bias = bsf_ref[...]
    for ci in range(tm // CHUNK):
        vchunk = vn[ci * CHUNK:(ci + 1) * CHUNK, :]
        mixed = jnp.zeros((CHUNK, B_WIDTH), F32)
        for g in range(B_GROUPS):
            r = jnp.dot(ws_ref[g], vchunk, preferred_element_type=F32)
            mixed = jnp.where(gid == g, r, mixed)
        ob_ref[0, ci * CHUNK:(ci + 1) * CHUNK, :] = (
            u[ci * CHUNK:(ci + 1) * CHUNK, :] * (mixed + bias)).astype(BF16)


def _project(x, shift, scale, gpre, win, gqk, bd, cos_a, sin_a, cos_c, sin_c, gvb, ws, bsf, tm):
    b, t, d = x.shape
    nt = t // tm
    full = lambda shape: pl.BlockSpec(shape, lambda bi, ti: (0,) * len(shape))
    per_b = pl.BlockSpec((1, 1, d), lambda bi, ti: (bi, 0, 0))
    tab = pl.BlockSpec((tm, LANES), lambda bi, ti: (ti, 0))
    out_shape = (
        jax.ShapeDtypeStruct((b, A_WIDTH, t), BF16),
        jax.ShapeDtypeStruct((b, t, A_KV_WIDTH), BF16),
        jax.ShapeDtypeStruct((b, nt, A_KV_WIDTH, tm), BF16),
        jax.ShapeDtypeStruct((b, t, B_WIDTH), BF16),
        jax.ShapeDtypeStruct((b, C_WIDTH, t), BF16),
        jax.ShapeDtypeStruct((b, t, C_WIDTH), BF16),
        jax.ShapeDtypeStruct((b, nt, C_WIDTH, tm), BF16),
    )
    out_specs = (
        pl.BlockSpec((1, A_WIDTH, tm), lambda bi, ti: (bi, 0, ti)),
        pl.BlockSpec((1, tm, A_KV_WIDTH), lambda bi, ti: (bi, ti, 0)),
        pl.BlockSpec((1, 1, A_KV_WIDTH, tm), lambda bi, ti: (bi, ti, 0, 0)),
        pl.BlockSpec((1, tm, B_WIDTH), lambda bi, ti: (bi, ti, 0)),
        pl.BlockSpec((1, C_WIDTH, tm), lambda bi, ti: (bi, 0, ti)),
        pl.BlockSpec((1, tm, C_WIDTH), lambda bi, ti: (bi, ti, 0)),
        pl.BlockSpec((1, 1, C_WIDTH, tm), lambda bi, ti: (bi, ti, 0, 0)),
    )
    return pl.pallas_call(
        _proj_kernel,
        out_shape=out_shape,
        grid=(b, nt),
        in_specs=[
            pl.BlockSpec((1, tm, d), lambda bi, ti: (bi, ti, 0)),
            per_b, per_b,
            full((1, d)),
            full((d, D_IN)),
            full((1, A_WIDTH + A_KV_WIDTH)),
            full((LANES, LANES)),
            tab, tab, tab, tab,
            full((1, B_WIDTH)),
            full((B_GROUPS, CHUNK, CHUNK)),
            full((CHUNK, B_WIDTH)),
        ],
        out_specs=out_specs,
        compiler_params=_cparams(("parallel", "parallel")),
        name="project",
    )(x, shift, scale, gpre, win, gqk, bd, cos_a, sin_a, cos_c, sin_c, gvb, ws, bsf)


def _padded_queries(qt, n_blocks, rows_per_block, pad_rows, row_of_block):
    tq = qt.shape[1]
    row = lax.broadcasted_iota(jnp.int32, (pad_rows, tq), 0)
    zero = jnp.zeros((pad_rows, tq), qt.dtype)
    cols = []
    for c in range(n_blocks):
        src = qt[c * rows_per_block:(c + 1) * rows_per_block, :]
        reps = pad_rows // rows_per_block
        tiled = jnp.concatenate([src] * reps, axis=0) if reps > 1 else src
        lo = row_of_block(c)
        keep = (row >= lo) & (row < lo + rows_per_block)
        cols.append(jnp.where(keep, tiled, zero))
    return jnp.concatenate(cols, axis=1)


def _attn_body(qpad, kv_sources, n_vgroups, tq):
    n_cols = qpad.shape[1]
    gcols = n_cols // n_vgroups

    def scores(k_blk):
        return jnp.dot(k_blk, qpad, preferred_element_type=F32)

    def pv(vt_blk, p):
        tk = vt_blk.shape[1]
        ones = jnp.ones((BF16_ROWS, tk), BF16)
        outs = []
        for g in range(n_vgroups):
            vext = jnp.concatenate([vt_blk[g * HEAD_DIM:(g + 1) * HEAD_DIM, :], ones], axis=0)
            outs.append(jnp.dot(vext, p[:, g * gcols:(g + 1) * gcols],
                                preferred_element_type=F32))
        return jnp.concatenate(outs, axis=1)

    def first(k_blk, vt_blk):
        s = scores(k_blk)
        m = jnp.max(s, axis=0, keepdims=True)
        p = jnp.exp2(s - m).astype(BF16)
        return m, pv(vt_blk, p)

    def update(k_blk, vt_blk, m, acc):
        s = scores(k_blk)
        m_new = jnp.maximum(m, jnp.max(s, axis=0, keepdims=True))
        alpha = jnp.exp2(m - m_new)
        p = jnp.exp2(s - m_new).astype(BF16)
        return m_new, alpha * acc + pv(vt_blk, p)

    m = acc = None
    for k_ref, vt_ref in kv_sources:
        nb, tk = vt_ref.shape[1], vt_ref.shape[3]
        if m is None:
            m, acc = first(k_ref[0, 0:tk, :], vt_ref[0, 0])
            start = 1
        else:
            start = 0
        if nb > start:
            def body(j, carry, k_ref=k_ref, vt_ref=vt_ref, tk=tk):
                off = pl.multiple_of(j * tk, tk)
                return update(k_ref[0, pl.ds(off, tk), :], vt_ref[0, j], *carry)
            m, acc = lax.fori_loop(start, nb, body, (m, acc))
    return acc


def _gqa_kernel(*refs, n_src):
    qt_ref = refs[0]
    kv = [(refs[1 + 2 * i], refs[2 + 2 * i]) for i in range(n_src)]
    o_ref = refs[1 + 2 * n_src]
    tq = qt_ref.shape[2]
    qpad = _padded_queries(qt_ref[0], A_HEADS, HEAD_DIM, A_KV_WIDTH,
                           lambda c: (c // A_GROUP) * HEAD_DIM)
    acc = _attn_body(qpad, kv, A_KV_HEADS, tq)
    o = acc[0:HEAD_DIM, :] / acc[HEAD_DIM:HEAD_DIM + 1, :]
    ot = jnp.concatenate([o[:, c * tq:(c + 1) * tq] for c in range(A_HEADS)], axis=0)
    o_ref[0] = ot.T.astype(o_ref.dtype)


def _diff_kernel(*refs, n_src, lam_init):
    qt_ref = refs[0]
    kv = [(refs[1 + 2 * i], refs[2 + 2 * i]) for i in range(n_src)]
    lamv_ref, gsub_ref, o_ref = refs[1 + 2 * n_src:4 + 2 * n_src]
    tq = qt_ref.shape[2]
    qpad = _padded_queries(qt_ref[0], 2 * C_HEADS, C_QK_DIM, C_WIDTH, lambda c: c * C_QK_DIM)
    acc = _attn_body(qpad, kv, C_HEADS, tq)
    lv = lamv_ref[...]
    lam = (jnp.exp(jnp.sum(lv[0:1] * lv[1:2], axis=1, keepdims=True))
           - jnp.exp(jnp.sum(lv[2:3] * lv[3:4], axis=1, keepdims=True)) + lam_init)
    a = acc[0:HEAD_DIM, :] / acc[HEAD_DIM:HEAD_DIM + 1, :]
    heads = []
    for hh in range(C_HEADS):
        d = a[:, (2 * hh) * tq:(2 * hh + 1) * tq] - lam * a[:, (2 * hh + 1) * tq:(2 * hh + 2) * tq]
        d = d * lax.rsqrt(jnp.mean(d * d, axis=0, keepdims=True) + EPS) * gsub_ref[...]
        heads.append(d * (1.0 - lam_init))
    o_ref[0] = jnp.concatenate(heads, axis=0).T.astype(o_ref.dtype)


def _attention(kind, qt, kv_sources, extra, lam_init=None):
    b, r, t = qt.shape
    tq = min(Q_TILE, t)
    in_specs = [pl.BlockSpec((1, r, tq), lambda bi, qi: (bi, 0, qi))]
    args = [qt]
    for k, vt in kv_sources:
        in_specs.append(pl.BlockSpec((1,) + k.shape[1:], lambda bi, qi: (bi, 0, 0)))
        in_specs.append(pl.BlockSpec((1,) + vt.shape[1:], lambda bi, qi: (bi, 0, 0, 0)))
        args += [k, vt]
    for e in extra:
        in_specs.append(pl.BlockSpec(e.shape, lambda bi, qi, n=e.ndim: (0,) * n))
        args.append(e)
    if kind == "gqa":
        width = A_WIDTH
        body = functools.partial(_gqa_kernel, n_src=len(kv_sources))
    else:
        width = C_WIDTH
        body = functools.partial(_diff_kernel, n_src=len(kv_sources), lam_init=lam_init)
    return pl.pallas_call(
        body,
        out_shape=jax.ShapeDtypeStruct((b, t, width), BF16),
        grid=(b, t // tq),
        in_specs=in_specs,
        out_specs=pl.BlockSpec((1, tq, width), lambda bi, qi: (bi, qi, 0)),
        compiler_params=_cparams(("parallel", "parallel")),
        name="attn_" + kind,
    )(*args)


def _outproj_kernel(x_ref, oa_ref, ob_ref, oc_ref, w_ref, gate_ref, gpost_ref, o_ref):
    mix = (jnp.dot(oa_ref[0], w_ref[0:A_WIDTH, :], preferred_element_type=F32)
           + jnp.dot(ob_ref[0], w_ref[A_WIDTH:A_WIDTH + B_WIDTH, :], preferred_element_type=F32)
           + jnp.dot(oc_ref[0], w_ref[A_WIDTH + B_WIDTH:, :], preferred_element_type=F32))
    o_ref[0] = x_ref[0] + gate_ref[0] * (_rms(mix) * gpost_ref[...])


def _out_project(x, oa, ob, oc, w_out, gate, gpost, tm):
    b, t, d = x.shape
    tok = lambda w: pl.BlockSpec((1, tm, w), lambda bi, ti: (bi, ti, 0))
    return pl.pallas_call(
        _outproj_kernel,
        out_shape=jax.ShapeDtypeStruct((b, t, d), F32),
        grid=(b, t // tm),
        in_specs=[
            tok(d), tok(A_WIDTH), tok(B_WIDTH), tok(C_WIDTH),
            pl.BlockSpec(w_out.shape, lambda bi, ti: (0, 0)),
            pl.BlockSpec((1, 1, d), lambda bi, ti: (bi, 0, 0)),
            pl.BlockSpec((1, d), lambda bi, ti: (0, 0)),
        ],
        out_specs=tok(d),
        compiler_params=_cparams(("parallel", "parallel")),
        name="out_project",
    )(x, oa, ob, oc, w_out, gate, gpost)


def _ffn_kernel(xc_ref, xp_ref, xn_ref, shift_ref, scale_ref, gate_ref, gpre_ref, gpost_ref,
                wup_ref, wconv_ref, bconv_ref, wdown_ref, o_ref, *, d_ff):
    ti, nt = pl.program_id(1), pl.num_programs(1)
    tm = xc_ref.shape[1]
    rows = tm + 2 * HALO
    xe = jnp.concatenate([xp_ref[0], xc_ref[0], xn_ref[0]], axis=0)
    h = _rms(xe) * gpre_ref[...]
    h = h * (1.0 + scale_ref[0]) + shift_ref[0]
    r = lax.broadcasted_iota(jnp.int32, (rows, 1), 0)
    lo = jnp.where(ti > 0, 0, HALO)
    hi = jnp.where(ti < nt - 1, rows, HALO + tm)
    valid = (r >= lo) & (r < hi)
    hb = jnp.where(valid, h, 0.0).astype(BF16)

    def conv(z, off):
        w = wconv_ref[:, off:off + FF_CHUNK]
        y = (pltpu.roll(z, 1, 0) * w[0:1] + z * w[1:2] + pltpu.roll(z, rows - 1, 0) * w[2:3]
             + bconv_ref[:, off:off + FF_CHUNK])
        return y[HALO:HALO + tm, :]

    acc = jnp.zeros((tm, o_ref.shape[2]), F32)
    for c in range(d_ff // FF_CHUNK):
        og, ov = c * FF_CHUNK, d_ff + c * FF_CHUNK
        zg = jnp.dot(hb, wup_ref[:, og:og + FF_CHUNK], preferred_element_type=F32)
        zv = jnp.dot(hb, wup_ref[:, ov:ov + FF_CHUNK], preferred_element_type=F32)
        act = jax.nn.silu(conv(zg, og)) * conv(zv, ov)
        acc = acc + jnp.dot(act.astype(BF16), wdown_ref[og:og + FF_CHUNK, :],
                            preferred_element_type=F32)
    o_ref[0] = xc_ref[0] + gate_ref[0] * (_rms(acc) * gpost_ref[...])


def _conv_ffn(x, shift, scale, gate, gpre, gpost, w_up, w_conv, b_conv, w_down, tm):
    b, t, d = x.shape
    d_ff = w_down.shape[0]
    nt = t // tm
    hb = tm // HALO
    last_hb = t // HALO - 1
    per_b = pl.BlockSpec((1, 1, d), lambda bi, ti: (bi, 0, 0))
    full = lambda a: pl.BlockSpec(a.shape, lambda bi, ti, n=a.ndim: (0,) * n)
    return pl.pallas_call(
        functools.partial(_ffn_kernel, d_ff=d_ff),
        out_shape=jax.ShapeDtypeStruct((b, t, d), F32),
        grid=(b, nt),
        in_specs=[
            pl.BlockSpec((1, tm, d), lambda bi, ti: (bi, ti, 0)),
            pl.BlockSpec((1, HALO, d), lambda bi, ti: (bi, jnp.maximum(ti * hb - 1, 0), 0)),
            pl.BlockSpec((1, HALO, d), lambda bi, ti: (bi, jnp.minimum((ti + 1) * hb, last_hb), 0)),
            per_b, per_b, per_b,
            full(gpre), full(gpost), full(w_up), full(w_conv), full(b_conv), full(w_down),
        ],
        out_specs=pl.BlockSpec((1, tm, d), lambda bi, ti: (bi, ti, 0)),
        compiler_params=_cparams(("parallel", "parallel")),
        name="conv_ffn",
    )(x, x, x, shift, scale, gate, gpre, gpost, w_up, w_conv, b_conv, w_down)


def _rope_tables(t, dim):
    rows = t // GRID_W
    row = jnp.repeat(jnp.arange(rows, dtype=F32), GRID_W)
    col = jnp.tile(jnp.arange(GRID_W, dtype=F32), rows)
    n_ax = dim // 4
    inv = ROPE_THETA ** (-jnp.arange(n_ax, dtype=F32) / n_ax)
    ang = jnp.concatenate([row[:, None] * inv, col[:, None] * inv], axis=-1)
    cos, sin = jnp.cos(ang), jnp.sin(ang)
    reps = LANES // dim
    return (jnp.tile(jnp.concatenate([cos, cos], axis=-1), (1, reps)),
            jnp.tile(jnp.concatenate([-sin, sin], axis=-1), (1, reps)))


def _identity_tables(t):
    return jnp.ones((t, LANES), F32), jnp.zeros((t, LANES), F32)


def kernel(x, c, ctx, c_ctx, w_mod, b_mod, g_pre_mix, g_post_mix, w_in, gq_a, gk_a, gv_b, w_s, b_s,
           lam_q1, lam_k1, lam_q2, lam_k2, g_sub_c, w_out, g_pre_ffn, g_post_ffn, w_up, w_conv,
           b_conv, w_down):
    bsz, t, d = x.shape
    tc = ctx.shape[1]
    depth = w_mod.shape[0]
    assert w_in.shape[-1] == D_IN, "unexpected input-projection width"
    tm_x = min(512, t)
    tm_c = min(512, tc)

    cond = jnp.zeros((SUBLANES, d), F32).at[0:bsz].set(c).at[bsz].set(c_ctx)
    mods = _modulation(cond, w_mod, b_mod)

    cos_a, sin_a = _rope_tables(t, HEAD_DIM)
    cos_c, sin_c = _rope_tables(t, C_QK_DIM)
    one_c, zero_c = _identity_tables(tc)
    lane = jnp.arange(LANES)
    bd = ((lane[:, None] // HEAD_DIM) == (lane[None, :] // HEAD_DIM)).astype(F32) / HEAD_DIM
    bd = bd.astype(BF16)

    xc = ctx
    for l in range(depth):
        last = l == depth - 1
        lam_init = 0.8 - 0.6 * math.exp(-0.3 * l)
        mx = [mods[l, m, 0:bsz][:, None, :] for m in range(N_MOD)]
        mc = [jnp.broadcast_to(mods[l, m, bsz][None, None, :], (bsz, 1, d)) for m in range(N_MOD)]
        row = lambda v: v.reshape(1, -1)
        win = w_in[l].astype(BF16)
        gqk = jnp.concatenate([jnp.tile(gq_a[l], A_HEADS), jnp.tile(gk_a[l], A_KV_HEADS)]).reshape(1, -1)
        ws = w_s[l].astype(BF16)
        bsf = jnp.repeat(b_s[l].T, HEAD_DIM, axis=1)
        wout = w_out[l].astype(BF16)
        lamv = jnp.zeros((SUBLANES, LANES), F32)
        lamv = lamv.at[0, 0:C_QK_DIM].set(lam_q1[l]).at[1, 0:C_QK_DIM].set(lam_k1[l])
        lamv = lamv.at[2, 0:C_QK_DIM].set(lam_q2[l]).at[3, 0:C_QK_DIM].set(lam_k2[l])
        gsub = g_sub_c[l].reshape(HEAD_DIM, 1)
        proj_args = (row(g_pre_mix[l]), win, gqk, bd)
        gate_args = (row(gv_b[l]), ws, bsf)

        px = _project(x, mx[0], mx[1], *proj_args, cos_a, sin_a, cos_c, sin_c, *gate_args, tm_x)
        pc = _project(xc, mc[0], mc[1], *proj_args, one_c, zero_c, one_c, zero_c, *gate_args, tm_c)
        qat_x, ka_x, vat_x, ob_x, qct_x, kc_x, vct_x = px
        qat_c, ka_c, vat_c, ob_c, qct_c, kc_c, vct_c = pc

        oa = _attention("gqa", qat_x, [(ka_c, vat_c), (ka_x, vat_x)], [])
        oc = _attention("diff", qct_x, [(kc_c, vct_c), (kc_x, vct_x)], [lamv, gsub], lam_init)
        x = _out_project(x, oa, ob_x, oc, wout, mx[2], row(g_post_mix[l]), tm_x)
        if not last:
            oa_c = _attention("gqa", qat_c, [(ka_c, vat_c)], [])
            oc_c = _attention("diff", qct_c, [(kc_c, vct_c)], [lamv, gsub], lam_init)
            xc = _out_project(xc, oa_c, ob_c, oc_c, wout, mc[2], row(g_post_mix[l]), tm_c)

        ffn_args = (row(g_pre_ffn[l]), row(g_post_ffn[l]), w_up[l].astype(BF16), w_conv[l],
                    row(b_conv[l]), w_down[l].astype(BF16))
        x = _conv_ffn(x, mx[3], mx[4], mx[5], *ffn_args, tm_x)
        if not last:
            xc = _conv_ffn(xc, mc[3], mc[4], mc[5], *ffn_args, tm_c)
    return x
```

```python
import functools
import math

import jax
import jax.numpy as jnp
from jax import lax
from jax.experimental import pallas as pl
from jax.experimental.pallas import tpu as pltpu

F32 = jnp.float32
BF16 = jnp.bfloat16

GRID_W = 64
HEAD_DIM = 64
A_HEADS = 8
A_KV_HEADS = 2
A_GROUP = A_HEADS // A_KV_HEADS
A_WIDTH = A_HEADS * HEAD_DIM
A_KV_WIDTH = A_KV_HEADS * HEAD_DIM
B_WIDTH = 256
B_GROUPS = B_WIDTH // HEAD_DIM
CHUNK = 128
C_HEADS = 4
C_WIDTH = C_HEADS * HEAD_DIM
C_QK_DIM = HEAD_DIM // 2
CONV_W = 3
ROPE_THETA = 10000.0
EPS = 1e-6
N_MOD = 6
LOG2E = 1.4426950408889634

LANES = 128
SUBLANES = 8
BF16_ROWS = 16
VMEM_LIMIT = 56 * 1024 * 1024
Q_TILE = 128
COL_TILE = 256
KEY_BLOCKS = (768, 512, 256)
SCORE_BUFFERS = 4
NEG_BIG = -1e30
FF_CHUNK = 256
HALO = SUBLANES

_OFF_QA = 0
_OFF_KA = _OFF_QA + A_WIDTH
_OFF_VA = _OFF_KA + A_KV_WIDTH
_OFF_UB = _OFF_VA + A_KV_WIDTH
_OFF_VB = _OFF_UB + B_WIDTH
_OFF_QC = _OFF_VB + B_WIDTH
_OFF_KC = _OFF_QC + C_WIDTH
_OFF_VC = _OFF_KC + C_WIDTH
D_IN = _OFF_VC + C_WIDTH


def _cparams(sem):
    return pltpu.CompilerParams(dimension_semantics=sem, vmem_limit_bytes=VMEM_LIMIT)


def _rms(v):
    return v * lax.rsqrt(jnp.mean(v * v, axis=-1, keepdims=True) + EPS)


def _mod_kernel(c_ref, w_ref, b_ref, o_ref):
    s = jax.nn.silu(c_ref[...])
    o_ref[0, 0] = jnp.dot(s, w_ref[0], preferred_element_type=F32,
                          precision=lax.Precision.HIGHEST) + b_ref[0, 0]


def _modulation(cond, w_mod, b_mod):
    n_layers, d, _ = w_mod.shape
    rows = cond.shape[0]
    b3 = b_mod.reshape(n_layers, N_MOD, 1, d)
    return pl.pallas_call(
        _mod_kernel,
        out_shape=jax.ShapeDtypeStruct((n_layers, N_MOD, rows, d), F32),
        grid=(n_layers, N_MOD),
        in_specs=[
            pl.BlockSpec((rows, d), lambda l, m: (0, 0)),
            pl.BlockSpec((1, d, d), lambda l, m: (l, 0, m)),
            pl.BlockSpec((1, 1, 1, d), lambda l, m: (l, m, 0, 0)),
        ],
        out_specs=pl.BlockSpec((1, 1, rows, d), lambda l, m: (l, m, 0, 0)),
        compiler_params=_cparams(("arbitrary", "arbitrary")),
        name="modulation",
    )(cond, w_mod, b3)


def _swap_halves(v, half):
    lane = lax.broadcasted_iota(jnp.int32, v.shape, 1)
    first = (lane % (2 * half)) < half
    return jnp.where(first, pltpu.roll(v, LANES - half, 1), pltpu.roll(v, half, 1))


def _proj_kernel(x_ref, shift_ref, scale_ref, gpre_ref, win_ref, gqk_ref, bd_ref,
                 cosa_ref, sina_ref, cosc_ref, sinc_ref, gvb_ref, ws_ref, bsf_ref,
                 qat_ref, ka_ref, vat_ref, ob_ref, qct_ref, kc_ref, vct_ref):
    tm = x_ref.shape[1]
    h = _rms(x_ref[0]) * gpre_ref[...]
    h = h * (1.0 + scale_ref[0]) + shift_ref[0]
    p = jnp.dot(h.astype(BF16), win_ref[...], preferred_element_type=F32)

    cos_a, sin_a = cosa_ref[...], sina_ref[...]
    bd = bd_ref[...]
    a_blocks = []
    for j in range((A_WIDTH + A_KV_WIDTH) // LANES):
        blk = p[:, j * LANES:(j + 1) * LANES]
        sq = blk * blk
        hi = sq.astype(BF16)
        lo = (sq - hi.astype(F32)).astype(BF16)
        msq = (jnp.dot(hi, bd, preferred_element_type=F32)
               + jnp.dot(lo, bd, preferred_element_type=F32))
        nb = blk * lax.rsqrt(msq + EPS) * gqk_ref[:, j * LANES:(j + 1) * LANES]
        a_blocks.append(nb * cos_a + _swap_halves(nb, HEAD_DIM // 2) * sin_a)
    qa = jnp.concatenate(a_blocks[:A_WIDTH // LANES], axis=1) * (HEAD_DIM ** -0.5 * LOG2E)
    qat_ref[0] = qa.T.astype(BF16)
    ka_ref[0] = a_blocks[A_WIDTH // LANES].astype(BF16)
    vat_ref[0] = p[:, _OFF_VA:_OFF_VA + A_KV_WIDTH].T.astype(BF16)

    cos_c, sin_c = cosc_ref[...], sinc_ref[...]

    def rope_c(off):
        blocks = []
        for j in range(C_WIDTH // LANES):
            blk = p[:, off + j * LANES:off + (j + 1) * LANES]
            blocks.append(blk * cos_c + _swap_halves(blk, C_QK_DIM // 2) * sin_c)
        return jnp.concatenate(blocks, axis=1)

    qct_ref[0] = (rope_c(_OFF_QC) * (C_QK_DIM ** -0.5 * LOG2E)).T.astype(BF16)
    kc_ref[0] = rope_c(_OFF_KC).astype(BF16)
    vct_ref[0] = p[:, _OFF_VC:_OFF_VC + C_WIDTH].T.astype(BF16)

    u = jax.nn.gelu(p[:, _OFF_UB:_OFF_UB + B_WIDTH])
    vn = (_rms(jax.nn.gelu(p[:, _OFF_VB:_OFF_VB + B_WIDTH])) * gvb_ref[...]).astype(BF16)
    gid = lax.broadcasted_iota(jnp.int32, (CHUNK, B_WIDTH), 1) // HEAD_DIM
    bias = bsf_ref[...]
    for ci in range(tm // CHUNK):
        vchunk = vn[ci * CHUNK:(ci + 1) * CHUNK, :]
        mixed = jnp.zeros((CHUNK, B_WIDTH), F32)
        for g in range(B_GROUPS):
            r = jnp.dot(ws_ref[g], vchunk, preferred_element_type=F32)
            mixed = jnp.where(gid == g, r, mixed)
        ob_ref[0, ci * CHUNK:(ci + 1) * CHUNK, :] = (
            u[ci * CHUNK:(ci + 1) * CHUNK, :] * (mixed + bias)).astype(BF16)


def _project(x, shift, scale, gpre, win, gqk, bd, cos_a, sin_a, cos_c, sin_c, gvb, ws, bsf, tm):
    b, t, d = x.shape
    nt = t // tm
    full = lambda shape: pl.BlockSpec(shape, lambda bi, ti: (0,) * len(shape))
    per_b = pl.BlockSpec((1, 1, d), lambda bi, ti: (bi, 0, 0))
    tab = pl.BlockSpec((tm, LANES), lambda bi, ti: (ti, 0))
    out_shape = (
        jax.ShapeDtypeStruct((b, A_WIDTH, t), BF16),
        jax.ShapeDtypeStruct((b, t, A_KV_WIDTH), BF16),
        jax.ShapeDtypeStruct((b, A_KV_WIDTH, t), BF16),
        jax.ShapeDtypeStruct((b, t, B_WIDTH), BF16),
        jax.ShapeDtypeStruct((b, C_WIDTH, t), BF16),
        jax.ShapeDtypeStruct((b, t, C_WIDTH), BF16),
        jax.ShapeDtypeStruct((b, C_WIDTH, t), BF16),
    )
    out_specs = (
        pl.BlockSpec((1, A_WIDTH, tm), lambda bi, ti: (bi, 0, ti)),
        pl.BlockSpec((1, tm, A_KV_WIDTH), lambda bi, ti: (bi, ti, 0)),
        pl.BlockSpec((1, A_KV_WIDTH, tm), lambda bi, ti: (bi, 0, ti)),
        pl.BlockSpec((1, tm, B_WIDTH), lambda bi, ti: (bi, ti, 0)),
        pl.BlockSpec((1, C_WIDTH, tm), lambda bi, ti: (bi, 0, ti)),
        pl.BlockSpec((1, tm, C_WIDTH), lambda bi, ti: (bi, ti, 0)),
        pl.BlockSpec((1, C_WIDTH, tm), lambda bi, ti: (bi, 0, ti)),
    )
    return pl.pallas_call(
        _proj_kernel,
        out_shape=out_shape,
        grid=(b, nt),
        in_specs=[
            pl.BlockSpec((1, tm, d), lambda bi, ti: (bi, ti, 0)),
            per_b, per_b,
            full((1, d)),
            full((d, D_IN)),
            full((1, A_WIDTH + A_KV_WIDTH)),
            full((LANES, LANES)),
            tab, tab, tab, tab,
            full((1, B_WIDTH)),
            full((B_GROUPS, CHUNK, CHUNK)),
            full((CHUNK, B_WIDTH)),
        ],
        out_specs=out_specs,
        compiler_params=_cparams(("parallel", "parallel")),
        name="project",
    )(x, shift, scale, gpre, win, gqk, bd, cos_a, sin_a, cos_c, sin_c, gvb, ws, bsf)


def _padded_queries(qt, n_blocks, rows_per_block, pad_rows, row_of_block):
    tq = qt.shape[1]
    row = lax.broadcasted_iota(jnp.int32, (pad_rows, tq), 0)
    zero = jnp.zeros((pad_rows, tq), qt.dtype)
    cols = []
    for c in range(n_blocks):
        src = qt[c * rows_per_block:(c + 1) * rows_per_block, :]
        reps = pad_rows // rows_per_block
        tiled = jnp.concatenate([src] * reps, axis=0) if reps > 1 else src
        lo = row_of_block(c)
        keep = (row >= lo) & (row < lo + rows_per_block)
        cols.append(jnp.where(keep, tiled, zero))
    return jnp.concatenate(cols, axis=1)


def _attn_pipeline(qpad_ref, k_ref, vt_ref, s_bufs, n_vgroups):
    nb, tk = vt_ref.shape[1], vt_ref.shape[3]
    n_cols = qpad_ref.shape[1]
    n_ct = n_cols // COL_TILE
    gcols = n_cols // n_vgroups
    ones = jnp.ones((BF16_ROWS, tk), BF16)

    def qk_tile(j, ct, s_ref):
        cs = slice(ct * COL_TILE, (ct + 1) * COL_TILE)
        kb = k_ref[0, pl.ds(pl.multiple_of(j * tk, tk), tk), :]
        s = jnp.dot(kb, qpad_ref[:, cs], preferred_element_type=F32)
        s_ref[:, cs] = s
        return jnp.max(s, axis=0, keepdims=True)

    def sm_tile(j, ct, s_ref, mblk, m, acc):
        cs = slice(ct * COL_TILE, (ct + 1) * COL_TILE)
        m_new = jnp.maximum(m, mblk)
        alpha = jnp.exp2(m - m_new)
        p = jnp.exp2(s_ref[:, cs] - m_new).astype(BF16)
        g = (ct * COL_TILE) // gcols
        vext = jnp.concatenate([vt_ref[0, j, g * HEAD_DIM:(g + 1) * HEAD_DIM, :], ones], axis=0)
        return m_new, alpha * acc + jnp.dot(vext, p, preferred_element_type=F32)

    def stage(j_qk, s_qk, j_sm, s_sm, mb_sm, ms, accs):
        mb_new, ms2, accs2 = [], [], []
        for ct in range(n_ct):
            if j_qk is not None:
                mb_new.append(qk_tile(j_qk, ct, s_qk))
            if j_sm is not None:
                m2, a2 = sm_tile(j_sm, ct, s_sm, mb_sm[ct], ms[ct], accs[ct])
                ms2.append(m2)
                accs2.append(a2)
        if j_sm is None:
            ms2, accs2 = ms, accs
        return tuple(mb_new), tuple(ms2), tuple(accs2)

    ms = tuple(jnp.full((1, COL_TILE), NEG_BIG, F32) for _ in range(n_ct))
    accs = tuple(jnp.zeros((HEAD_DIM + BF16_ROWS, COL_TILE), F32) for _ in range(n_ct))
    nbuf = len(s_bufs)
    mb, _, _ = stage(0, s_bufs[0], None, None, None, ms, accs)
    n_steps = nb - 1
    n_iter = n_steps // nbuf
    if n_iter > 0:
        def body(i, carry):
            ms, accs, mb = carry
            for u in range(nbuf):
                j = i * nbuf + u
                mb, ms, accs = stage(j + 1, s_bufs[(u + 1) % nbuf], j, s_bufs[u], mb, ms, accs)
            return ms, accs, mb
        ms, accs, mb = lax.fori_loop(0, n_iter, body, (ms, accs, mb))
    for j in range(n_iter * nbuf, n_steps):
        mb, ms, accs = stage(j + 1, s_bufs[(j + 1) % nbuf], j, s_bufs[j % nbuf], mb, ms, accs)
    _, ms, accs = stage(None, None, nb - 1, s_bufs[(nb - 1) % nbuf], mb, ms, accs)
    return jnp.concatenate(accs, axis=1)


def _gqa_kernel(qt_ref, k_ref, vt_ref, o_ref, qpad_ref, *s_bufs):
    tq = qt_ref.shape[2]
    qpad_ref[...] = _padded_queries(qt_ref[0], A_HEADS, HEAD_DIM, A_KV_WIDTH,
                                    lambda c: (c // A_GROUP) * HEAD_DIM)
    acc = _attn_pipeline(qpad_ref, k_ref, vt_ref, s_bufs, A_KV_HEADS)
    o = acc[0:HEAD_DIM, :] / acc[HEAD_DIM:HEAD_DIM + 1, :]
    ot = jnp.concatenate([o[:, c * tq:(c + 1) * tq] for c in range(A_HEADS)], axis=0)
    o_ref[0] = ot.T.astype(o_ref.dtype)


def _diff_kernel(qt_ref, k_ref, vt_ref, lamv_ref, gsub_ref, o_ref, qpad_ref, *s_bufs, lam_init):
    tq = qt_ref.shape[2]
    qpad_ref[...] = _padded_queries(qt_ref[0], 2 * C_HEADS, C_QK_DIM, C_WIDTH,
                                    lambda c: c * C_QK_DIM)
    acc = _attn_pipeline(qpad_ref, k_ref, vt_ref, s_bufs, C_HEADS)
    lv = lamv_ref[...]
    lam = (jnp.exp(jnp.sum(lv[0:1] * lv[1:2], axis=1, keepdims=True))
           - jnp.exp(jnp.sum(lv[2:3] * lv[3:4], axis=1, keepdims=True)) + lam_init)
    a = acc[0:HEAD_DIM, :] / acc[HEAD_DIM:HEAD_DIM + 1, :]
    heads = []
    for hh in range(C_HEADS):
        d = a[:, (2 * hh) * tq:(2 * hh + 1) * tq] - lam * a[:, (2 * hh + 1) * tq:(2 * hh + 2) * tq]
        d = d * lax.rsqrt(jnp.mean(d * d, axis=0, keepdims=True) + EPS) * gsub_ref[...]
        heads.append(d * (1.0 - lam_init))
    o_ref[0] = jnp.concatenate(heads, axis=0).T.astype(o_ref.dtype)


def _key_block(s):
    return next(tk for tk in KEY_BLOCKS if s % tk == 0)


def _attention(kind, qt, k, vt, extra, lam_init=None):
    b, r, t = qt.shape
    s, wk = k.shape[1:]
    wv = vt.shape[1]
    tq = min(Q_TILE, t)
    tk = _key_block(s)
    nb = s // tk
    vt_blocks = jnp.transpose(vt.reshape(b, wv, nb, tk), (0, 2, 1, 3))
    n_cols = 2 * C_HEADS * tq
    in_specs = [pl.BlockSpec((1, r, tq), lambda bi, qi: (bi, 0, qi)),
                pl.BlockSpec((1, s, wk), lambda bi, qi: (bi, 0, 0)),
                pl.BlockSpec((1, nb, wv, tk), lambda bi, qi: (bi, 0, 0, 0))]
    for e in extra:
        in_specs.append(pl.BlockSpec(e.shape, lambda bi, qi, n=e.ndim: (0,) * n))
    if kind == "gqa":
        width = A_WIDTH
        body = _gqa_kernel
    else:
        width = C_WIDTH
        body = functools.partial(_diff_kernel, lam_init=lam_init)
    scratch = ([pltpu.VMEM((wk, n_cols), BF16)]
               + [pltpu.VMEM((tk, n_cols), F32)] * min(SCORE_BUFFERS, nb))
    return pl.pallas_call(
        body,
        out_shape=jax.ShapeDtypeStruct((b, t, width), BF16),
        grid=(b, t // tq),
        in_specs=in_specs,
        out_specs=pl.BlockSpec((1, tq, width), lambda bi, qi: (bi, qi, 0)),
        scratch_shapes=scratch,
        compiler_params=_cparams(("parallel", "parallel")),
        name="attn_" + kind,
    )(qt, k, vt_blocks, *extra)


def _outproj_kernel(x_ref, oa_ref, ob_ref, oc_ref, w_ref, gate_ref, gpost_ref, o_ref):
    mix = (jnp.dot(oa_ref[0], w_ref[0:A_WIDTH, :], preferred_element_type=F32)
           + jnp.dot(ob_ref[0], w_ref[A_WIDTH:A_WIDTH + B_WIDTH, :], preferred_element_type=F32)
           + jnp.dot(oc_ref[0], w_ref[A_WIDTH + B_WIDTH:, :], preferred_element_type=F32))
    o_ref[0] = x_ref[0] + gate_ref[0] * (_rms(mix) * gpost_ref[...])


def _out_project(x, oa, ob, oc, w_out, gate, gpost, tm):
    b, t, d = x.shape
    tok = lambda w: pl.BlockSpec((1, tm, w), lambda bi, ti: (bi, ti, 0))
    return pl.pallas_call(
        _outproj_kernel,
        out_shape=jax.ShapeDtypeStruct((b, t, d), F32),
        grid=(b, t // tm),
        in_specs=[
            tok(d), tok(A_WIDTH), tok(B_WIDTH), tok(C_WIDTH),
            pl.BlockSpec(w_out.shape, lambda bi, ti: (0, 0)),
            pl.BlockSpec((1, 1, d), lambda bi, ti: (bi, 0, 0)),
            pl.BlockSpec((1, d), lambda bi, ti: (0, 0)),
        ],
        out_specs=tok(d),
        compiler_params=_cparams(("parallel", "parallel")),
        name="out_project",
    )(x, oa, ob, oc, w_out, gate, gpost)


def _ffn_kernel(xc_ref, xp_ref, xn_ref, shift_ref, scale_ref, gate_ref, gpre_ref, gpost_ref,
                wup_ref, wconv_ref, bconv_ref, wdown_ref, o_ref, *, d_ff):
    ti, nt = pl.program_id(1), pl.num_programs(1)
    tm = xc_ref.shape[1]
    rows = tm + 2 * HALO
    xe = jnp.concatenate([xp_ref[0], xc_ref[0], xn_ref[0]], axis=0)
    h = _rms(xe) * gpre_ref[...]
    h = h * (1.0 + scale_ref[0]) + shift_ref[0]
    r = lax.broadcasted_iota(jnp.int32, (rows, 1), 0)
    lo = jnp.where(ti > 0, 0, HALO)
    hi = jnp.where(ti < nt - 1, rows, HALO + tm)
    valid = (r >= lo) & (r < hi)
    hb = jnp.where(valid, h, 0.0).astype(BF16)

    def conv(z, off):
        w = wconv_ref[:, off:off + FF_CHUNK]
        y = (pltpu.roll(z, 1, 0) * w[0:1] + z * w[1:2] + pltpu.roll(z, rows - 1, 0) * w[2:3]
             + bconv_ref[:, off:off + FF_CHUNK])
        return y[HALO:HALO + tm, :]

    acc = jnp.zeros((tm, o_ref.shape[2]), F32)
    for c in range(d_ff // FF_CHUNK):
        og, ov = c * FF_CHUNK, d_ff + c * FF_CHUNK
        zg = jnp.dot(hb, wup_ref[:, og:og + FF_CHUNK], preferred_element_type=F32)
        zv = jnp.dot(hb, wup_ref[:, ov:ov + FF_CHUNK], preferred_element_type=F32)
        act = jax.nn.silu(conv(zg, og)) * conv(zv, ov)
        acc = acc + jnp.dot(act.astype(BF16), wdown_ref[og:og + FF_CHUNK, :],
                            preferred_element_type=F32)
    o_ref[0] = xc_ref[0] + gate_ref[0] * (_rms(acc) * gpost_ref[...])


def _conv_ffn(x, shift, scale, gate, gpre, gpost, w_up, w_conv, b_conv, w_down, tm):
    b, t, d = x.shape
    d_ff = w_down.shape[0]
    nt = t // tm
    hb = tm // HALO
    last_hb = t // HALO - 1
    per_b = pl.BlockSpec((1, 1, d), lambda bi, ti: (bi, 0, 0))
    full = lambda a: pl.BlockSpec(a.shape, lambda bi, ti, n=a.ndim: (0,) * n)
    return pl.pallas_call(
        functools.partial(_ffn_kernel, d_ff=d_ff),
        out_shape=jax.ShapeDtypeStruct((b, t, d), F32),
        grid=(b, nt),
        in_specs=[
            pl.BlockSpec((1, tm, d), lambda bi, ti: (bi, ti, 0)),
            pl.BlockSpec((1, HALO, d), lambda bi, ti: (bi, jnp.maximum(ti * hb - 1, 0), 0)),
            pl.BlockSpec((1, HALO, d), lambda bi, ti: (bi, jnp.minimum((ti + 1) * hb, last_hb), 0)),
            per_b, per_b, per_b,
            full(gpre), full(gpost), full(w_up), full(w_conv), full(b_conv), full(w_down),
        ],
        out_specs=pl.BlockSpec((1, tm, d), lambda bi, ti: (bi, ti, 0)),
        compiler_params=_cparams(("parallel", "parallel")),
        name="conv_ffn",
    )(x, x, x, shift, scale, gate, gpre, gpost, w_up, w_conv, b_conv, w_down)


def _rope_tables(t, dim):
    rows = t // GRID_W
    row = jnp.repeat(jnp.arange(rows, dtype=F32), GRID_W)
    col = jnp.tile(jnp.arange(GRID_W, dtype=F32), rows)
    n_ax = dim // 4
    inv = ROPE_THETA ** (-jnp.arange(n_ax, dtype=F32) / n_ax)
    ang = jnp.concatenate([row[:, None] * inv, col[:, None] * inv], axis=-1)
    cos, sin = jnp.cos(ang), jnp.sin(ang)
    reps = LANES // dim
    return (jnp.tile(jnp.concatenate([cos, cos], axis=-1), (1, reps)),
            jnp.tile(jnp.concatenate([-sin, sin], axis=-1), (1, reps)))


def _identity_tables(t):
    return jnp.ones((t, LANES), F32), jnp.zeros((t, LANES), F32)


def kernel(x, c, ctx, c_ctx, w_mod, b_mod, g_pre_mix, g_post_mix, w_in, gq_a, gk_a, gv_b, w_s, b_s,
           lam_q1, lam_k1, lam_q2, lam_k2, g_sub_c, w_out, g_pre_ffn, g_post_ffn, w_up, w_conv,
           b_conv, w_down):
    bsz, t, d = x.shape
    tc = ctx.shape[1]
    depth = w_mod.shape[0]
    assert w_in.shape[-1] == D_IN, "unexpected input-projection width"
    tm_x = min(512, t)
    tm_c = min(512, tc)

    cond = jnp.zeros((SUBLANES, d), F32).at[0:bsz].set(c).at[bsz].set(c_ctx)
    mods = _modulation(cond, w_mod, b_mod)

    cos_a, sin_a = _rope_tables(t, HEAD_DIM)
    cos_c, sin_c = _rope_tables(t, C_QK_DIM)
    one_c, zero_c = _identity_tables(tc)
    lane = jnp.arange(LANES)
    bd = ((lane[:, None] // HEAD_DIM) == (lane[None, :] // HEAD_DIM)).astype(F32) / HEAD_DIM
    bd = bd.astype(BF16)

    xc = ctx
    for l in range(depth):
        last = l == depth - 1
        lam_init = 0.8 - 0.6 * math.exp(-0.3 * l)
        mx = [mods[l, m, 0:bsz][:, None, :] for m in range(N_MOD)]
        mc = [jnp.broadcast_to(mods[l, m, bsz][None, None, :], (bsz, 1, d)) for m in range(N_MOD)]
        row = lambda v: v.reshape(1, -1)
        win = w_in[l].astype(BF16)
        gqk = jnp.concatenate([jnp.tile(gq_a[l], A_HEADS), jnp.tile(gk_a[l], A_KV_HEADS)]).reshape(1, -1)
        ws = w_s[l].astype(BF16)
        bsf = jnp.repeat(b_s[l].T, HEAD_DIM, axis=1)
        wout = w_out[l].astype(BF16)
        lamv = jnp.zeros((SUBLANES, LANES), F32)
        lamv = lamv.at[0, 0:C_QK_DIM].set(lam_q1[l]).at[1, 0:C_QK_DIM].set(lam_k1[l])
        lamv = lamv.at[2, 0:C_QK_DIM].set(lam_q2[l]).at[3, 0:C_QK_DIM].set(lam_k2[l])
        gsub = g_sub_c[l].reshape(HEAD_DIM, 1)
        proj_args = (row(g_pre_mix[l]), win, gqk, bd)
        gate_args = (row(gv_b[l]), ws, bsf)

        px = _project(x, mx[0], mx[1], *proj_args, cos_a, sin_a, cos_c, sin_c, *gate_args, tm_x)
        pc = _project(xc, mc[0], mc[1], *proj_args, one_c, zero_c, one_c, zero_c, *gate_args, tm_c)
        qat_x, ka_x, vat_x, ob_x, qct_x, kc_x, vct_x = px
        qat_c, ka_c, vat_c, ob_c, qct_c, kc_c, vct_c = pc

        cat = lambda a_c, a_x, axis: jnp.concatenate([a_c, a_x], axis=axis)
        oa = _attention("gqa", qat_x, cat(ka_c, ka_x, 1), cat(vat_c, vat_x, 2), [])
        oc = _attention("diff", qct_x, cat(kc_c, kc_x, 1), cat(vct_c, vct_x, 2), [lamv, gsub], lam_init)
        x = _out_project(x, oa, ob_x, oc, wout, mx[2], row(g_post_mix[l]), tm_x)
        if not last:
            oa_c = _attention("gqa", qat_c, ka_c, vat_c, [])
            oc_c = _attention("diff", qct_c, kc_c, vct_c, [lamv, gsub], lam_init)
            xc = _out_project(xc, oa_c, ob_c, oc_c, wout, mc[2], row(g_post_mix[l]), tm_c)

        ffn_args = (row(g_pre_ffn[l]), row(g_post_ffn[l]), w_up[l].astype(BF16), w_conv[l],
                    row(b_conv[l]), w_down[l].astype(BF16))
        x = _conv_ffn(x, mx[3], mx[4], mx[5], *ffn_args, tm_x)
        if not last:
            xc = _conv_ffn(xc, mc[3], mc[4], mc[5], *ffn_args, tm_c)
    return x
```

```python
import functools
import math

import jax
import jax.numpy as jnp
import numpy as np
from jax import lax
from jax.experimental import pallas as pl
from jax.experimental.pallas import tpu as pltpu

F32 = jnp.float32
BF16 = jnp.bfloat16

GRID_W = 64
HEAD_DIM = 64
A_HEADS = 8
A_KV_HEADS = 2
A_GROUP = A_HEADS // A_KV_HEADS
A_WIDTH = A_HEADS * HEAD_DIM
A_KV_WIDTH = A_KV_HEADS * HEAD_DIM
B_WIDTH = 256
B_GROUPS = B_WIDTH // HEAD_DIM
CHUNK = 128
C_HEADS = 4
C_WIDTH = C_HEADS * HEAD_DIM
C_QK_DIM = HEAD_DIM // 2
CONV_W = 3
ROPE_THETA = 10000.0
EPS = 1e-6
N_MOD = 6
LOG2E = 1.4426950408889634

LANES = 128
SUBLANES = 8
BF16_ROWS = 16
VMEM_LIMIT = 56 * 1024 * 1024
PROJ_ROWS = 256
Q_TILE = 256
COL_TILE = 256
KEY_BLOCKS = (768, 512, 256)
SCORE_BUFFERS = 4
NEG_BIG = -1e30
FF_CHUNK = 256
HALO = SUBLANES

_OFF_QA = 0
_OFF_KA = _OFF_QA + A_WIDTH
_OFF_VA = _OFF_KA + A_KV_WIDTH
_OFF_UB = _OFF_VA + A_KV_WIDTH
_OFF_VB = _OFF_UB + B_WIDTH
_OFF_QC = _OFF_VB + B_WIDTH
_OFF_KC = _OFF_QC + C_WIDTH
_OFF_VC = _OFF_KC + C_WIDTH
D_IN = _OFF_VC + C_WIDTH


def _cparams(sem):
    return pltpu.CompilerParams(dimension_semantics=sem, vmem_limit_bytes=VMEM_LIMIT)


def _rms(v):
    return v * lax.rsqrt(jnp.mean(v * v, axis=-1, keepdims=True) + EPS)


def _mod_kernel(c_ref, w_ref, b_ref, o_ref):
    s = jax.nn.silu(c_ref[...])
    o_ref[0, 0] = jnp.dot(s, w_ref[0], preferred_element_type=F32,
                          precision=lax.Precision.HIGHEST) + b_ref[0, 0]


def _modulation(cond, w_mod, b_mod):
    n_layers, d, _ = w_mod.shape
    rows = cond.shape[0]
    b3 = b_mod.reshape(n_layers, N_MOD, 1, d)
    return pl.pallas_call(
        _mod_kernel,
        out_shape=jax.ShapeDtypeStruct((n_layers, N_MOD, rows, d), F32),
        grid=(n_layers, N_MOD),
        in_specs=[
            pl.BlockSpec((rows, d), lambda l, m: (0, 0)),
            pl.BlockSpec((1, d, d), lambda l, m: (l, 0, m)),
            pl.BlockSpec((1, 1, 1, d), lambda l, m: (l, m, 0, 0)),
        ],
        out_specs=pl.BlockSpec((1, 1, rows, d), lambda l, m: (l, m, 0, 0)),
        compiler_params=_cparams(("arbitrary", "arbitrary")),
        name="modulation",
    )(cond, w_mod, b3)


def _swap_halves(v, half):
    lane = lax.broadcasted_iota(jnp.int32, v.shape, 1)
    first = (lane % (2 * half)) < half
    return jnp.where(first, pltpu.roll(v, LANES - half, 1), pltpu.roll(v, half, 1))


def _proj_kernel(x_ref, shift_ref, scale_ref, gpre_ref, win_ref, gqk_ref, bd_ref,
                 cosa_ref, sina_ref, cosc_ref, sinc_ref, gvb_ref, ws_ref, bsf_ref,
                 qat_ref, ka_ref, vat_ref, ob_ref, qct_ref, kc_ref, vct_ref):
    tm = x_ref.shape[1]
    rg = min(PROJ_ROWS, tm)
    for r0 in range(0, tm, rg):
        _proj_rows(slice(r0, r0 + rg), x_ref, shift_ref, scale_ref, gpre_ref, win_ref, gqk_ref,
                   bd_ref, cosa_ref, sina_ref, cosc_ref, sinc_ref, gvb_ref, ws_ref, bsf_ref,
                   qat_ref, ka_ref, vat_ref, ob_ref, qct_ref, kc_ref, vct_ref)


def _proj_rows(rs, x_ref, shift_ref, scale_ref, gpre_ref, win_ref, gqk_ref, bd_ref,
               cosa_ref, sina_ref, cosc_ref, sinc_ref, gvb_ref, ws_ref, bsf_ref,
               qat_ref, ka_ref, vat_ref, ob_ref, qct_ref, kc_ref, vct_ref):
    tm = rs.stop - rs.start
    h = _rms(x_ref[0, rs, :]) * gpre_ref[...]
    h = h * (1.0 + scale_ref[0]) + shift_ref[0]
    p = jnp.dot(h.astype(BF16), win_ref[...], preferred_element_type=F32)

    cos_a, sin_a = cosa_ref[rs, :], sina_ref[rs, :]
    bd = bd_ref[...]
    a_blocks = []
    for j in range((A_WIDTH + A_KV_WIDTH) // LANES):
        blk = p[:, j * LANES:(j + 1) * LANES]
        sq = blk * blk
        hi = sq.astype(BF16)
        lo = (sq - hi.astype(F32)).astype(BF16)
        msq = (jnp.dot(hi, bd, preferred_element_type=F32)
               + jnp.dot(lo, bd, preferred_element_type=F32))
        nb = blk * lax.rsqrt(msq + EPS) * gqk_ref[:, j * LANES:(j + 1) * LANES]
        a_blocks.append(nb * cos_a + _swap_halves(nb, HEAD_DIM // 2) * sin_a)
    qa = jnp.concatenate(a_blocks[:A_WIDTH // LANES], axis=1) * (HEAD_DIM ** -0.5 * LOG2E)
    qat_ref[0, :, rs] = qa.T.astype(BF16)
    ka_ref[0, rs, :] = a_blocks[A_WIDTH // LANES].astype(BF16)
    vat_ref[0, :, rs] = p[:, _OFF_VA:_OFF_VA + A_KV_WIDTH].T.astype(BF16)

    cos_c, sin_c = cosc_ref[rs, :], sinc_ref[rs, :]

    def rope_c(off):
        blocks = []
        for j in range(C_WIDTH // LANES):
            blk = p[:, off + j * LANES:off + (j + 1) * LANES]
            blocks.append(blk * cos_c + _swap_halves(blk, C_QK_DIM // 2) * sin_c)
        return jnp.concatenate(blocks, axis=1)

    qct_ref[0, :, rs] = (rope_c(_OFF_QC) * (C_QK_DIM ** -0.5 * LOG2E)).T.astype(BF16)
    kc_ref[0, rs, :] = rope_c(_OFF_KC).astype(BF16)
    vct_ref[0, :, rs] = p[:, _OFF_VC:_OFF_VC + C_WIDTH].T.astype(BF16)

    u = jax.nn.gelu(p[:, _OFF_UB:_OFF_UB + B_WIDTH])
    vn = (_rms(jax.nn.gelu(p[:, _OFF_VB:_OFF_VB + B_WIDTH])) * gvb_ref[...]).astype(BF16)
    gid = lax.broadcasted_iota(jnp.int32, (CHUNK, B_WIDTH), 1) // HEAD_DIM
    bias = bsf_ref[...]
    for ci in range(tm // CHUNK):
        vchunk = vn[ci * CHUNK:(ci + 1) * CHUNK, :]
        mixed = jnp.zeros((CHUNK, B_WIDTH), F32)
        for g in range(B_GROUPS):
            r = jnp.dot(ws_ref[g], vchunk, preferred_element_type=F32)
            mixed = jnp.where(gid == g, r, mixed)
        ob_ref[0, rs.start + ci * CHUNK:rs.start + (ci + 1) * CHUNK, :] = (
            u[ci * CHUNK:(ci + 1) * CHUNK, :] * (mixed + bias)).astype(BF16)


def _mod_spec(mods, mod):
    first, step = mod
    return pl.BlockSpec((1, 1, mods.shape[-1]), lambda bi, ti: (first + step * bi, 0, 0))


def _project(x, mods, shift, scale, gpre, win, gqk, bd, cos_a, sin_a, cos_c, sin_c, gvb, ws, bsf, tm):
    b, t, d = x.shape
    nt = t // tm
    full = lambda shape: pl.BlockSpec(shape, lambda bi, ti: (0,) * len(shape))
    tab = pl.BlockSpec((tm, LANES), lambda bi, ti: (ti, 0))
    out_shape = (
        jax.ShapeDtypeStruct((b, A_WIDTH, t), BF16),
        jax.ShapeDtypeStruct((b, t, A_KV_WIDTH), BF16),
        jax.ShapeDtypeStruct((b, A_KV_WIDTH, t), BF16),
        jax.ShapeDtypeStruct((b, t, B_WIDTH), BF16),
        jax.ShapeDtypeStruct((b, C_WIDTH, t), BF16),
        jax.ShapeDtypeStruct((b, t, C_WIDTH), BF16),
        jax.ShapeDtypeStruct((b, C_WIDTH, t), BF16),
    )
    out_specs = (
        pl.BlockSpec((1, A_WIDTH, tm), lambda bi, ti: (bi, 0, ti)),
        pl.BlockSpec((1, tm, A_KV_WIDTH), lambda bi, ti: (bi, ti, 0)),
        pl.BlockSpec((1, A_KV_WIDTH, tm), lambda bi, ti: (bi, 0, ti)),
        pl.BlockSpec((1, tm, B_WIDTH), lambda bi, ti: (bi, ti, 0)),
        pl.BlockSpec((1, C_WIDTH, tm), lambda bi, ti: (bi, 0, ti)),
        pl.BlockSpec((1, tm, C_WIDTH), lambda bi, ti: (bi, ti, 0)),
        pl.BlockSpec((1, C_WIDTH, tm), lambda bi, ti: (bi, 0, ti)),
    )
    return pl.pallas_call(
        _proj_kernel,
        out_shape=out_shape,
        grid=(b, nt),
        in_specs=[
            pl.BlockSpec((1, tm, d), lambda bi, ti: (bi, ti, 0)),
            _mod_spec(mods, shift), _mod_spec(mods, scale),
            full((1, d)),
            full((d, D_IN)),
            full((1, A_WIDTH + A_KV_WIDTH)),
            full((LANES, LANES)),
            tab, tab, tab, tab,
            full((1, B_WIDTH)),
            full((B_GROUPS, CHUNK, CHUNK)),
            full((CHUNK, B_WIDTH)),
        ],
        out_specs=out_specs,
        compiler_params=_cparams(("parallel", "parallel")),
        name="project",
    )(x, mods, mods, gpre, win, gqk, bd, cos_a, sin_a, cos_c, sin_c, gvb, ws, bsf)


def _padded_queries(qt, n_blocks, rows_per_block, pad_rows, row_of_block):
    tq = qt.shape[1]
    row = lax.broadcasted_iota(jnp.int32, (pad_rows, tq), 0)
    zero = jnp.zeros((pad_rows, tq), qt.dtype)
    cols = []
    for c in range(n_blocks):
        src = qt[c * rows_per_block:(c + 1) * rows_per_block, :]
        reps = pad_rows // rows_per_block
        tiled = jnp.concatenate([src] * reps, axis=0) if reps > 1 else src
        lo = row_of_block(c)
        keep = (row >= lo) & (row < lo + rows_per_block)
        cols.append(jnp.where(keep, tiled, zero))
    return jnp.concatenate(cols, axis=1)


def _attn_pipeline(qpad_ref, k_ref, vt_ref, s_bufs, n_vgroups):
    nb, tk = vt_ref.shape[1], vt_ref.shape[3]
    n_cols = qpad_ref.shape[1]
    n_ct = n_cols // COL_TILE
    gcols = n_cols // n_vgroups
    ones = jnp.ones((BF16_ROWS, tk), BF16)

    def qk_tile(j, ct, s_ref):
        cs = slice(ct * COL_TILE, (ct + 1) * COL_TILE)
        kb = k_ref[0, pl.ds(pl.multiple_of(j * tk, tk), tk), :]
        s = jnp.dot(kb, qpad_ref[:, cs], preferred_element_type=F32)
        s_ref[:, cs] = s
        return jnp.max(s, axis=0, keepdims=True)

    def sm_tile(j, ct, s_ref, mblk, m, acc):
        cs = slice(ct * COL_TILE, (ct + 1) * COL_TILE)
        m_new = jnp.maximum(m, mblk)
        alpha = jnp.exp2(m - m_new)
        p = jnp.exp2(s_ref[:, cs] - m_new).astype(BF16)
        g = (ct * COL_TILE) // gcols
        vext = jnp.concatenate([vt_ref[0, j, g * HEAD_DIM:(g + 1) * HEAD_DIM, :], ones], axis=0)
        return m_new, alpha * acc + jnp.dot(vext, p, preferred_element_type=F32)

    def stage(j_qk, s_qk, j_sm, s_sm, mb_sm, ms, accs):
        mb_new, ms2, accs2 = [], [], []
        for ct in range(n_ct):
            if j_qk is not None:
                mb_new.append(qk_tile(j_qk, ct, s_qk))
            if j_sm is not None:
                m2, a2 = sm_tile(j_sm, ct, s_sm, mb_sm[ct], ms[ct], accs[ct])
                ms2.append(m2)
                accs2.append(a2)
        if j_sm is None:
            ms2, accs2 = ms, accs
        return tuple(mb_new), tuple(ms2), tuple(accs2)

    ms = tuple(jnp.full((1, COL_TILE), NEG_BIG, F32) for _ in range(n_ct))
    accs = tuple(jnp.zeros((HEAD_DIM + BF16_ROWS, COL_TILE), F32) for _ in range(n_ct))
    nbuf = len(s_bufs)
    mb, _, _ = stage(0, s_bufs[0], None, None, None, ms, accs)
    n_steps = nb - 1
    n_iter = n_steps // nbuf
    if n_iter > 0:
        def body(i, carry):
            ms, accs, mb = carry
            for u in range(nbuf):
                j = i * nbuf + u
                mb, ms, accs = stage(j + 1, s_bufs[(u + 1) % nbuf], j, s_bufs[u], mb, ms, accs)
            return ms, accs, mb
        ms, accs, mb = lax.fori_loop(0, n_iter, body, (ms, accs, mb))
    for j in range(n_iter * nbuf, n_steps):
        mb, ms, accs = stage(j + 1, s_bufs[(j + 1) % nbuf], j, s_bufs[j % nbuf], mb, ms, accs)
    _, ms, accs = stage(None, None, nb - 1, s_bufs[(nb - 1) % nbuf], mb, ms, accs)
    return jnp.concatenate(accs, axis=1)


def _gqa_kernel(qt_ref, k_ref, vt_ref, o_ref, qpad_ref, *s_bufs):
    tq = qt_ref.shape[2]
    qpad_ref[...] = _padded_queries(qt_ref[0], A_HEADS, HEAD_DIM, A_KV_WIDTH,
                                    lambda c: (c // A_GROUP) * HEAD_DIM)
    acc = _attn_pipeline(qpad_ref, k_ref, vt_ref, s_bufs, A_KV_HEADS)
    o = acc[0:HEAD_DIM, :] / acc[HEAD_DIM:HEAD_DIM + 1, :]
    ot = jnp.concatenate([o[:, c * tq:(c + 1) * tq] for c in range(A_HEADS)], axis=0)
    o_ref[0] = ot.T.astype(o_ref.dtype)


def _diff_kernel(qt_ref, k_ref, vt_ref, lamv_ref, gsub_ref, o_ref, qpad_ref, *s_bufs, lam_init):
    tq = qt_ref.shape[2]
    qpad_ref[...] = _padded_queries(qt_ref[0], 2 * C_HEADS, C_QK_DIM, C_WIDTH,
                                    lambda c: c * C_QK_DIM)
    acc = _attn_pipeline(qpad_ref, k_ref, vt_ref, s_bufs, C_HEADS)
    lv = lamv_ref[...]
    lam = (jnp.exp(jnp.sum(lv[0:1] * lv[1:2], axis=1, keepdims=True))
           - jnp.exp(jnp.sum(lv[2:3] * lv[3:4], axis=1, keepdims=True)) + lam_init)
    a = acc[0:HEAD_DIM, :] / acc[HEAD_DIM:HEAD_DIM + 1, :]
    heads = []
    for hh in range(C_HEADS):
        d = a[:, (2 * hh) * tq:(2 * hh + 1) * tq] - lam * a[:, (2 * hh + 1) * tq:(2 * hh + 2) * tq]
        d = d * lax.rsqrt(jnp.mean(d * d, axis=0, keepdims=True) + EPS) * gsub_ref[...]
        heads.append(d * (1.0 - lam_init))
    o_ref[0] = jnp.concatenate(heads, axis=0).T.astype(o_ref.dtype)


def _key_block(s):
    return next(tk for tk in KEY_BLOCKS if s % tk == 0)


def _attention(kind, qt, k, vt, extra, lam_init=None):
    b, r, t = qt.shape
    s, wk = k.shape[1:]
    wv = vt.shape[1]
    tq = min(Q_TILE, t)
    tk = _key_block(s)
    nb = s // tk
    vt_blocks = jnp.transpose(vt.reshape(b, wv, nb, tk), (0, 2, 1, 3))
    n_cols = 2 * C_HEADS * tq
    in_specs = [pl.BlockSpec((1, r, tq), lambda bi, qi: (bi, 0, qi)),
                pl.BlockSpec((1, s, wk), lambda bi, qi: (bi, 0, 0)),
                pl.BlockSpec((1, nb, wv, tk), lambda bi, qi: (bi, 0, 0, 0))]
    for e in extra:
        in_specs.append(pl.BlockSpec(e.shape, lambda bi, qi, n=e.ndim: (0,) * n))
    if kind == "gqa":
        width = A_WIDTH
        body = _gqa_kernel
    else:
        width = C_WIDTH
        body = functools.partial(_diff_kernel, lam_init=lam_init)
    scratch = ([pltpu.VMEM((wk, n_cols), BF16)]
               + [pltpu.VMEM((tk, n_cols), F32)] * min(SCORE_BUFFERS, nb))
    return pl.pallas_call(
        body,
        out_shape=jax.ShapeDtypeStruct((b, t, width), BF16),
        grid=(b, t // tq),
        in_specs=in_specs,
        out_specs=pl.BlockSpec((1, tq, width), lambda bi, qi: (bi, qi, 0)),
        scratch_shapes=scratch,
        compiler_params=_cparams(("parallel", "parallel")),
        name="attn_" + kind,
    )(qt, k, vt_blocks, *extra)


def _outproj_kernel(x_ref, oa_ref, ob_ref, oc_ref, w_ref, gate_ref, gpost_ref, o_ref):
    mix = (jnp.dot(oa_ref[0], w_ref[0:A_WIDTH, :], preferred_element_type=F32)
           + jnp.dot(ob_ref[0], w_ref[A_WIDTH:A_WIDTH + B_WIDTH, :], preferred_element_type=F32)
           + jnp.dot(oc_ref[0], w_ref[A_WIDTH + B_WIDTH:, :], preferred_element_type=F32))
    o_ref[0] = x_ref[0] + gate_ref[0] * (_rms(mix) * gpost_ref[...])


def _out_project(x, oa, ob, oc, w_out, mods, gate, gpost, tm):
    b, t, d = x.shape
    tok = lambda w: pl.BlockSpec((1, tm, w), lambda bi, ti: (bi, ti, 0))
    return pl.pallas_call(
        _outproj_kernel,
        out_shape=jax.ShapeDtypeStruct((b, t, d), F32),
        grid=(b, t // tm),
        in_specs=[
            tok(d), tok(A_WIDTH), tok(B_WIDTH), tok(C_WIDTH),
            pl.BlockSpec(w_out.shape, lambda bi, ti: (0, 0)),
            _mod_spec(mods, gate),
            pl.BlockSpec((1, d), lambda bi, ti: (0, 0)),
        ],
        out_specs=tok(d),
        compiler_params=_cparams(("parallel", "parallel")),
        name="out_project",
    )(x, oa, ob, oc, w_out, mods, gpost)


def _ffn_kernel(xc_ref, xp_ref, xn_ref, shift_ref, scale_ref, gate_ref, gpre_ref, gpost_ref,
                wup_ref, wconv_ref, bconv_ref, wdown_ref, o_ref, act_ref, *, d_ff):
    ti, nt = pl.program_id(1), pl.num_programs(1)
    tm = xc_ref.shape[1]
    rows = tm + 2 * HALO
    xe = jnp.concatenate([xp_ref[0], xc_ref[0], xn_ref[0]], axis=0)
    h = _rms(xe) * gpre_ref[...]
    h = h * (1.0 + scale_ref[0]) + shift_ref[0]
    r = lax.broadcasted_iota(jnp.int32, (rows, 1), 0)
    lo = jnp.where(ti > 0, 0, HALO)
    hi = jnp.where(ti < nt - 1, rows, HALO + tm)
    valid = (r >= lo) & (r < hi)
    hb = jnp.where(valid, h, 0.0).astype(BF16)

    def conv(z, off):
        w = wconv_ref[:, off:off + FF_CHUNK]
        y = (pltpu.roll(z, 1, 0) * w[0:1] + z * w[1:2] + pltpu.roll(z, rows - 1, 0) * w[2:3]
             + bconv_ref[:, off:off + FF_CHUNK])
        return y[HALO:HALO + tm, :]

    for c in range(d_ff // FF_CHUNK):
        og, ov = c * FF_CHUNK, d_ff + c * FF_CHUNK
        zg = jnp.dot(hb, wup_ref[:, og:og + FF_CHUNK], preferred_element_type=F32)
        zv = jnp.dot(hb, wup_ref[:, ov:ov + FF_CHUNK], preferred_element_type=F32)
        act_ref[:, og:og + FF_CHUNK] = (jax.nn.silu(conv(zg, og)) * conv(zv, ov)).astype(BF16)
    f = jnp.dot(act_ref[...], wdown_ref[...], preferred_element_type=F32)
    o_ref[0] = xc_ref[0] + gate_ref[0] * (_rms(f) * gpost_ref[...])


def _conv_ffn(x, mods, shift, scale, gate, gpre, gpost, w_up, w_conv, b_conv, w_down, tm):
    b, t, d = x.shape
    d_ff = w_down.shape[0]
    nt = t // tm
    hb = tm // HALO
    last_hb = t // HALO - 1
    full = lambda a: pl.BlockSpec(a.shape, lambda bi, ti, n=a.ndim: (0,) * n)
    return pl.pallas_call(
        functools.partial(_ffn_kernel, d_ff=d_ff),
        out_shape=jax.ShapeDtypeStruct((b, t, d), F32),
        grid=(b, nt),
        in_specs=[
            pl.BlockSpec((1, tm, d), lambda bi, ti: (bi, ti, 0)),
            pl.BlockSpec((1, HALO, d), lambda bi, ti: (bi, jnp.maximum(ti * hb - 1, 0), 0)),
            pl.BlockSpec((1, HALO, d), lambda bi, ti: (bi, jnp.minimum((ti + 1) * hb, last_hb), 0)),
            _mod_spec(mods, shift), _mod_spec(mods, scale), _mod_spec(mods, gate),
            full(gpre), full(gpost), full(w_up), full(w_conv), full(b_conv), full(w_down),
        ],
        out_specs=pl.BlockSpec((1, tm, d), lambda bi, ti: (bi, ti, 0)),
        scratch_shapes=[pltpu.VMEM((tm, d_ff), BF16)],
        compiler_params=_cparams(("parallel", "parallel")),
        name="conv_ffn",
    )(x, x, x, mods, mods, mods, gpre, gpost, w_up, w_conv, b_conv, w_down)


def _rope_tables(t, dim):
    rows = t // GRID_W
    row = np.repeat(np.arange(rows, dtype=np.float32), GRID_W)
    col = np.tile(np.arange(GRID_W, dtype=np.float32), rows)
    n_ax = dim // 4
    inv = (ROPE_THETA ** (-np.arange(n_ax, dtype=np.float32) / n_ax)).astype(np.float32)
    ang = np.concatenate([row[:, None] * inv, col[:, None] * inv], axis=-1)
    cos, sin = np.cos(ang), np.sin(ang)
    reps = LANES // dim
    return (jnp.asarray(np.tile(np.concatenate([cos, cos], axis=-1), (1, reps)), F32),
            jnp.asarray(np.tile(np.concatenate([-sin, sin], axis=-1), (1, reps)), F32))


def _identity_tables(t):
    return jnp.ones((t, LANES), F32), jnp.zeros((t, LANES), F32)


def kernel(x, c, ctx, c_ctx, w_mod, b_mod, g_pre_mix, g_post_mix, w_in, gq_a, gk_a, gv_b, w_s, b_s,
           lam_q1, lam_k1, lam_q2, lam_k2, g_sub_c, w_out, g_pre_ffn, g_post_ffn, w_up, w_conv,
           b_conv, w_down):
    bsz, t, d = x.shape
    tc = ctx.shape[1]
    depth = w_mod.shape[0]
    assert w_in.shape[-1] == D_IN, "unexpected input-projection width"
    tm_x = min(512, t)
    tm_c = min(512, tc)

    cond = jnp.concatenate([c, c_ctx[None, :], jnp.zeros((SUBLANES - bsz - 1, d), F32)], axis=0)
    mods = _modulation(cond, w_mod, b_mod).reshape(depth * N_MOD * SUBLANES, 1, d)
    mod_x = lambda l, m: ((l * N_MOD + m) * SUBLANES, 1)
    mod_c = lambda l, m: ((l * N_MOD + m) * SUBLANES + bsz, 0)

    cos_a, sin_a = _rope_tables(t, HEAD_DIM)
    cos_c, sin_c = _rope_tables(t, C_QK_DIM)
    one_c, zero_c = _identity_tables(tc)
    lane = np.arange(LANES)
    bd = jnp.asarray(((lane[:, None] // HEAD_DIM) == (lane[None, :] // HEAD_DIM)) / HEAD_DIM, BF16)

    xc = ctx
    for l in range(depth):
        last = l == depth - 1
        lam_init = 0.8 - 0.6 * math.exp(-0.3 * l)
        mx = [mod_x(l, m) for m in range(N_MOD)]
        mc = [mod_c(l, m) for m in range(N_MOD)]
        row = lambda v: v.reshape(1, -1)
        win = w_in[l].astype(BF16)
        gqk = jnp.concatenate([jnp.tile(gq_a[l], A_HEADS), jnp.tile(gk_a[l], A_KV_HEADS)]).reshape(1, -1)
        ws = w_s[l].astype(BF16)
        bsf = jnp.repeat(b_s[l].T, HEAD_DIM, axis=1)
        wout = w_out[l].astype(BF16)
        lamv = jnp.pad(jnp.stack([lam_q1[l], lam_k1[l], lam_q2[l], lam_k2[l]]),
                       ((0, SUBLANES - 4), (0, LANES - C_QK_DIM)))
        gsub = g_sub_c[l].reshape(HEAD_DIM, 1)
        proj_args = (row(g_pre_mix[l]), win, gqk, bd)
        gate_args = (row(gv_b[l]), ws, bsf)

        px = _project(x, mods, mx[0], mx[1], *proj_args, cos_a, sin_a, cos_c, sin_c, *gate_args, tm_x)
        pc = _project(xc, mods, mc[0], mc[1], *proj_args, one_c, zero_c, one_c, zero_c, *gate_args, tm_c)
        qat_x, ka_x, vat_x, ob_x, qct_x, kc_x, vct_x = px
        qat_c, ka_c, vat_c, ob_c, qct_c, kc_c, vct_c = pc

        cat = lambda a_c, a_x, axis: jnp.concatenate([a_c, a_x], axis=axis)
        oa = _attention("gqa", qat_x, cat(ka_c, ka_x, 1), cat(vat_c, vat_x, 2), [])
        oc = _attention("diff", qct_x, cat(kc_c, kc_x, 1), cat(vct_c, vct_x, 2), [lamv, gsub], lam_init)
        x = _out_project(x, oa, ob_x, oc, wout, mods, mx[2], row(g_post_mix[l]), tm_x)
        if not last:
            oa_c = _attention("gqa", qat_c, ka_c, vat_c, [])
            oc_c = _attention("diff", qct_c, kc_c, vct_c, [lamv, gsub], lam_init)
            xc = _out_project(xc, oa_c, ob_c, oc_c, wout, mods, mc[2], row(g_post_mix[l]), tm_c)

        ffn_args = (row(g_pre_ffn[l]), row(g_post_ffn[l]), w_up[l].astype(BF16), w_conv[l],
                    row(b_conv[l]), w_down[l].astype(BF16))
        x = _conv_ffn(x, mods, mx[3], mx[4], mx[5], *ffn_args, tm_x)
        if not last:
            xc = _conv_ffn(xc, mods, mc[3], mc[4], mc[5], *ffn_args, tm_c)
    return x
```

```python
import functools
import math

import jax
import jax.numpy as jnp
import numpy as np
from jax import lax
from jax.experimental import pallas as pl
from jax.experimental.pallas import tpu as pltpu

F32 = jnp.float32
BF16 = jnp.bfloat16

GRID_W = 64
HEAD_DIM = 64
A_HEADS = 8
A_KV_HEADS = 2
A_GROUP = A_HEADS // A_KV_HEADS
A_WIDTH = A_HEADS * HEAD_DIM
A_KV_WIDTH = A_KV_HEADS * HEAD_DIM
B_WIDTH = 256
B_GROUPS = B_WIDTH // HEAD_DIM
CHUNK = 128
C_HEADS = 4
C_WIDTH = C_HEADS * HEAD_DIM
C_QK_DIM = HEAD_DIM // 2
CONV_W = 3
ROPE_THETA = 10000.0
EPS = 1e-6
N_MOD = 6
LOG2E = 1.4426950408889634

LANES = 128
SUBLANES = 8
BF16_ROWS = 16
VMEM_LIMIT = 56 * 1024 * 1024
PROJ_ROWS = 256
Q_TILE = 256
COL_TILE = 256
KEY_BLOCKS = (768, 512, 256)
SCORE_BUFFERS = 4
NEG_BIG = -1e30
FF_CHUNK = 256
HALO = SUBLANES

_OFF_QA = 0
_OFF_KA = _OFF_QA + A_WIDTH
_OFF_VA = _OFF_KA + A_KV_WIDTH
_OFF_UB = _OFF_VA + A_KV_WIDTH
_OFF_VB = _OFF_UB + B_WIDTH
_OFF_QC = _OFF_VB + B_WIDTH
_OFF_KC = _OFF_QC + C_WIDTH
_OFF_VC = _OFF_KC + C_WIDTH
D_IN = _OFF_VC + C_WIDTH


def _cparams(sem):
    return pltpu.CompilerParams(dimension_semantics=sem, vmem_limit_bytes=VMEM_LIMIT)


def _rms(v):
    return v * lax.rsqrt(jnp.mean(v * v, axis=-1, keepdims=True) + EPS)


def _mod_kernel(c_ref, w_ref, b_ref, o_ref):
    s = jax.nn.silu(c_ref[...])
    o_ref[0, 0] = jnp.dot(s, w_ref[0], preferred_element_type=F32) + b_ref[0, 0]


def _modulation(cond, w_mod, b_mod):
    n_layers, d, _ = w_mod.shape
    rows = cond.shape[0]
    b3 = b_mod.reshape(n_layers, N_MOD, 1, d)
    return pl.pallas_call(
        _mod_kernel,
        out_shape=jax.ShapeDtypeStruct((n_layers, N_MOD, rows, d), F32),
        grid=(n_layers, N_MOD),
        in_specs=[
            pl.BlockSpec((rows, d), lambda l, m: (0, 0)),
            pl.BlockSpec((1, d, d), lambda l, m: (l, 0, m)),
            pl.BlockSpec((1, 1, 1, d), lambda l, m: (l, m, 0, 0)),
        ],
        out_specs=pl.BlockSpec((1, 1, rows, d), lambda l, m: (l, m, 0, 0)),
        compiler_params=_cparams(("arbitrary", "arbitrary")),
        name="modulation",
    )(cond, w_mod, b3)


def _swap_halves(v, half):
    lane = lax.broadcasted_iota(jnp.int32, v.shape, 1)
    first = (lane % (2 * half)) < half
    return jnp.where(first, pltpu.roll(v, LANES - half, 1), pltpu.roll(v, half, 1))


def _proj_kernel(x_ref, shift_ref, scale_ref, gpre_ref, win_ref, gqk_ref, bd_ref,
                 cosa_ref, sina_ref, cosc_ref, sinc_ref, gvb_ref, ws_ref, bsf_ref,
                 qat_ref, ka_ref, vat_ref, ob_ref, qct_ref, kc_ref, vct_ref):
    tm = x_ref.shape[1]
    rg = min(PROJ_ROWS, tm)
    for r0 in range(0, tm, rg):
        _proj_rows(slice(r0, r0 + rg), x_ref, shift_ref, scale_ref, gpre_ref, win_ref, gqk_ref,
                   bd_ref, cosa_ref, sina_ref, cosc_ref, sinc_ref, gvb_ref, ws_ref, bsf_ref,
                   qat_ref, ka_ref, vat_ref, ob_ref, qct_ref, kc_ref, vct_ref)


def _proj_rows(rs, x_ref, shift_ref, scale_ref, gpre_ref, win_ref, gqk_ref, bd_ref,
               cosa_ref, sina_ref, cosc_ref, sinc_ref, gvb_ref, ws_ref, bsf_ref,
               qat_ref, ka_ref, vat_ref, ob_ref, qct_ref, kc_ref, vct_ref):
    tm = rs.stop - rs.start
    h = _rms(x_ref[0, rs, :]) * gpre_ref[...]
    h = h * (1.0 + scale_ref[0]) + shift_ref[0]
    p = jnp.dot(h.astype(BF16), win_ref[...], preferred_element_type=F32)

    cos_a, sin_a = cosa_ref[rs, :], sina_ref[rs, :]
    bd = bd_ref[...]
    a_blocks = []
    for j in range((A_WIDTH + A_KV_WIDTH) // LANES):
        blk = p[:, j * LANES:(j + 1) * LANES]
        sq = blk * blk
        hi = sq.astype(BF16)
        lo = (sq - hi.astype(F32)).astype(BF16)
        msq = (jnp.dot(hi, bd, preferred_element_type=F32)
               + jnp.dot(lo, bd, preferred_element_type=F32))
        nb = blk * lax.rsqrt(msq + EPS) * gqk_ref[:, j * LANES:(j + 1) * LANES]
        a_blocks.append(nb * cos_a + _swap_halves(nb, HEAD_DIM // 2) * sin_a)
    qa = jnp.concatenate(a_blocks[:A_WIDTH // LANES], axis=1) * (HEAD_DIM ** -0.5 * LOG2E)
    qat_ref[0, :, rs] = qa.T.astype(BF16)
    ka_ref[0, rs, :] = a_blocks[A_WIDTH // LANES].astype(BF16)
    vat_ref[0, :, rs] = p[:, _OFF_VA:_OFF_VA + A_KV_WIDTH].T.astype(BF16)

    cos_c, sin_c = cosc_ref[rs, :], sinc_ref[rs, :]

    def rope_c(off):
        blocks = []
        for j in range(C_WIDTH // LANES):
            blk = p[:, off + j * LANES:off + (j + 1) * LANES]
            blocks.append(blk * cos_c + _swap_halves(blk, C_QK_DIM // 2) * sin_c)
        return jnp.concatenate(blocks, axis=1)

    qct_ref[0, :, rs] = (rope_c(_OFF_QC) * (C_QK_DIM ** -0.5 * LOG2E)).T.astype(BF16)
    kc_ref[0, rs, :] = rope_c(_OFF_KC).astype(BF16)
    vct_ref[0, :, rs] = p[:, _OFF_VC:_OFF_VC + C_WIDTH].T.astype(BF16)

    u = jax.nn.gelu(p[:, _OFF_UB:_OFF_UB + B_WIDTH])
    vn = (_rms(jax.nn.gelu(p[:, _OFF_VB:_OFF_VB + B_WIDTH])) * gvb_ref[...]).astype(BF16)
    gid = lax.broadcasted_iota(jnp.int32, (CHUNK, B_WIDTH), 1) // HEAD_DIM
    bias = bsf_ref[...]
    for ci in range(tm // CHUNK):
        vchunk = vn[ci * CHUNK:(ci + 1) * CHUNK, :]
        mixed = jnp.zeros((CHUNK, B_WIDTH), F32)
        for g in range(B_GROUPS):
            r = jnp.dot(ws_ref[g], vchunk, preferred_element_type=F32)
            mixed = jnp.where(gid == g, r, mixed)
        ob_ref[0, rs.start + ci * CHUNK:rs.start + (ci + 1) * CHUNK, :] = (
            u[ci * CHUNK:(ci + 1) * CHUNK, :] * (mixed + bias)).astype(BF16)


def _mod_spec(mods, mod):
    first, step = mod
    return pl.BlockSpec((1, 1, mods.shape[-1]), lambda bi, ti: (first + step * bi, 0, 0))


def _project(x, mods, shift, scale, gpre, win, gqk, bd, cos_a, sin_a, cos_c, sin_c, gvb, ws, bsf, tm):
    b, t, d = x.shape
    nt = t // tm
    full = lambda shape: pl.BlockSpec(shape, lambda bi, ti: (0,) * len(shape))
    tab = pl.BlockSpec((tm, LANES), lambda bi, ti: (ti, 0))
    out_shape = (
        jax.ShapeDtypeStruct((b, A_WIDTH, t), BF16),
        jax.ShapeDtypeStruct((b, t, A_KV_WIDTH), BF16),
        jax.ShapeDtypeStruct((b, A_KV_WIDTH, t), BF16),
        jax.ShapeDtypeStruct((b, t, B_WIDTH), BF16),
        jax.ShapeDtypeStruct((b, C_WIDTH, t), BF16),
        jax.ShapeDtypeStruct((b, t, C_WIDTH), BF16),
        jax.ShapeDtypeStruct((b, C_WIDTH, t), BF16),
    )
    out_specs = (
        pl.BlockSpec((1, A_WIDTH, tm), lambda bi, ti: (bi, 0, ti)),
        pl.BlockSpec((1, tm, A_KV_WIDTH), lambda bi, ti: (bi, ti, 0)),
        pl.BlockSpec((1, A_KV_WIDTH, tm), lambda bi, ti: (bi, 0, ti)),
        pl.BlockSpec((1, tm, B_WIDTH), lambda bi, ti: (bi, ti, 0)),
        pl.BlockSpec((1, C_WIDTH, tm), lambda bi, ti: (bi, 0, ti)),
        pl.BlockSpec((1, tm, C_WIDTH), lambda bi, ti: (bi, ti, 0)),
        pl.BlockSpec((1, C_WIDTH, tm), lambda bi, ti: (bi, 0, ti)),
    )
    return pl.pallas_call(
        _proj_kernel,
        out_shape=out_shape,
        grid=(b, nt),
        in_specs=[
            pl.BlockSpec((1, tm, d), lambda bi, ti: (bi, ti, 0)),
            _mod_spec(mods, shift), _mod_spec(mods, scale),
            full((1, d)),
            full((d, D_IN)),
            full((1, A_WIDTH + A_KV_WIDTH)),
            full((LANES, LANES)),
            tab, tab, tab, tab,
            full((1, B_WIDTH)),
            full((B_GROUPS, CHUNK, CHUNK)),
            full((CHUNK, B_WIDTH)),
        ],
        out_specs=out_specs,
        compiler_params=_cparams(("parallel", "parallel")),
        name="project",
    )(x, mods, mods, gpre, win, gqk, bd, cos_a, sin_a, cos_c, sin_c, gvb, ws, bsf)


def _padded_queries(qt, n_blocks, rows_per_block, pad_rows, row_of_block):
    tq = qt.shape[1]
    row = lax.broadcasted_iota(jnp.int32, (pad_rows, tq), 0)
    zero = jnp.zeros((pad_rows, tq), qt.dtype)
    cols = []
    for c in range(n_blocks):
        src = qt[c * rows_per_block:(c + 1) * rows_per_block, :]
        reps = pad_rows // rows_per_block
        tiled = jnp.concatenate([src] * reps, axis=0) if reps > 1 else src
        lo = row_of_block(c)
        keep = (row >= lo) & (row < lo + rows_per_block)
        cols.append(jnp.where(keep, tiled, zero))
    return jnp.concatenate(cols, axis=1)


def _attn_pipeline(qpad_ref, k_ref, vt_ref, s_bufs, n_vgroups):
    nb, tk = vt_ref.shape[1], vt_ref.shape[3]
    n_cols = qpad_ref.shape[1]
    n_ct = n_cols // COL_TILE
    gcols = n_cols // n_vgroups
    ones = jnp.ones((BF16_ROWS, tk), BF16)

    def qk_tile(j, ct, s_ref):
        cs = slice(ct * COL_TILE, (ct + 1) * COL_TILE)
        kb = k_ref[0, pl.ds(pl.multiple_of(j * tk, tk), tk), :]
        s = jnp.dot(kb, qpad_ref[:, cs], preferred_element_type=F32)
        s_ref[ct] = s
        return jnp.max(s, axis=0, keepdims=True)

    def sm_tile(j, ct, s_ref, mblk, m, acc):
        m_new = jnp.maximum(m, mblk)
        alpha = jnp.exp2(m - m_new)
        p = jnp.exp2(s_ref[ct] - m_new).astype(BF16)
        g = (ct * COL_TILE) // gcols
        vext = jnp.concatenate([vt_ref[0, j, g * HEAD_DIM:(g + 1) * HEAD_DIM, :], ones], axis=0)
        return m_new, alpha * acc + jnp.dot(vext, p, preferred_element_type=F32)

    def stage(j_qk, s_qk, j_sm, s_sm, mb_sm, ms, accs):
        mb_new, ms2, accs2 = [], [], []
        for ct in range(n_ct):
            if j_qk is not None:
                mb_new.append(qk_tile(j_qk, ct, s_qk))
            if j_sm is not None:
                m2, a2 = sm_tile(j_sm, ct, s_sm, mb_sm[ct], ms[ct], accs[ct])
                ms2.append(m2)
                accs2.append(a2)
        if j_sm is None:
            ms2, accs2 = ms, accs
        return tuple(mb_new), tuple(ms2), tuple(accs2)

    ms = tuple(jnp.full((1, COL_TILE), NEG_BIG, F32) for _ in range(n_ct))
    accs = tuple(jnp.zeros((HEAD_DIM + BF16_ROWS, COL_TILE), F32) for _ in range(n_ct))
    nbuf = len(s_bufs)
    mb, _, _ = stage(0, s_bufs[0], None, None, None, ms, accs)
    n_steps = nb - 1
    n_iter = n_steps // nbuf
    if n_iter > 0:
        def body(i, carry):
            ms, accs, mb = carry
            for u in range(nbuf):
                j = i * nbuf + u
                mb, ms, accs = stage(j + 1, s_bufs[(u + 1) % nbuf], j, s_bufs[u], mb, ms, accs)
            return ms, accs, mb
        ms, accs, mb = lax.fori_loop(0, n_iter, body, (ms, accs, mb))
    for j in range(n_iter * nbuf, n_steps):
        mb, ms, accs = stage(j + 1, s_bufs[(j + 1) % nbuf], j, s_bufs[j % nbuf], mb, ms, accs)
    _, ms, accs = stage(None, None, nb - 1, s_bufs[(nb - 1) % nbuf], mb, ms, accs)
    return jnp.concatenate(accs, axis=1)


def _gqa_kernel(qt_ref, k_ref, vt_ref, o_ref, qpad_ref, *s_bufs):
    tq = qt_ref.shape[2]
    qpad_ref[...] = _padded_queries(qt_ref[0], A_HEADS, HEAD_DIM, A_KV_WIDTH,
                                    lambda c: (c // A_GROUP) * HEAD_DIM)
    acc = _attn_pipeline(qpad_ref, k_ref, vt_ref, s_bufs, A_KV_HEADS)
    o = acc[0:HEAD_DIM, :] / acc[HEAD_DIM:HEAD_DIM + 1, :]
    ot = jnp.concatenate([o[:, c * tq:(c + 1) * tq] for c in range(A_HEADS)], axis=0)
    o_ref[0] = ot.T.astype(o_ref.dtype)


def _diff_kernel(qt_ref, k_ref, vt_ref, lamv_ref, gsub_ref, x_ref, oa_ref, ob_ref, wout_ref,
                 gate_ref, gpost_ref, o_ref, qpad_ref, *s_bufs, lam_init):
    tq = qt_ref.shape[2]
    qpad_ref[...] = _padded_queries(qt_ref[0], 2 * C_HEADS, C_QK_DIM, C_WIDTH,
                                    lambda c: c * C_QK_DIM)
    acc = _attn_pipeline(qpad_ref, k_ref, vt_ref, s_bufs, C_HEADS)
    lv = lamv_ref[...]
    lam = (jnp.exp(jnp.sum(lv[0:1] * lv[1:2], axis=1, keepdims=True))
           - jnp.exp(jnp.sum(lv[2:3] * lv[3:4], axis=1, keepdims=True)) + lam_init)
    a = acc[0:HEAD_DIM, :] / acc[HEAD_DIM:HEAD_DIM + 1, :]
    heads = []
    for hh in range(C_HEADS):
        d = a[:, (2 * hh) * tq:(2 * hh + 1) * tq] - lam * a[:, (2 * hh + 1) * tq:(2 * hh + 2) * tq]
        d = d * lax.rsqrt(jnp.mean(d * d, axis=0, keepdims=True) + EPS) * gsub_ref[...]
        heads.append(d * (1.0 - lam_init))
    oc = jnp.concatenate(heads, axis=0).T.astype(BF16)
    mix = (jnp.dot(oa_ref[0], wout_ref[0:A_WIDTH, :], preferred_element_type=F32)
           + jnp.dot(ob_ref[0], wout_ref[A_WIDTH:A_WIDTH + B_WIDTH, :], preferred_element_type=F32)
           + jnp.dot(oc, wout_ref[A_WIDTH + B_WIDTH:, :], preferred_element_type=F32))
    o_ref[0] = x_ref[0] + gate_ref[0] * (_rms(mix) * gpost_ref[...])


def _key_block(s):
    return next(tk for tk in KEY_BLOCKS if s % tk == 0)


def _attention(body, name, qt, k, vt, extra_args, extra_specs, out_width, out_dtype):
    b, r, t = qt.shape
    s, wk = k.shape[1:]
    wv = vt.shape[1]
    tq = min(Q_TILE, t)
    tk = _key_block(s)
    nb = s // tk
    vt_blocks = jnp.transpose(vt.reshape(b, wv, nb, tk), (0, 2, 1, 3))
    n_cols = 2 * C_HEADS * tq
    once = pl.Buffered(1)
    in_specs = [pl.BlockSpec((1, r, tq), lambda bi, qi: (bi, 0, qi)),
                pl.BlockSpec((1, s, wk), lambda bi, qi: (bi, 0, 0), pipeline_mode=once),
                pl.BlockSpec((1, nb, wv, tk), lambda bi, qi: (bi, 0, 0, 0), pipeline_mode=once)]
    scratch = ([pltpu.VMEM((wk, n_cols), BF16)]
               + [pltpu.VMEM((n_cols // COL_TILE, tk, COL_TILE), F32)] * min(SCORE_BUFFERS, nb))
    return pl.pallas_call(
        body,
        out_shape=jax.ShapeDtypeStruct((b, t, out_width), out_dtype),
        grid=(b, t // tq),
        in_specs=in_specs + list(extra_specs),
        out_specs=pl.BlockSpec((1, tq, out_width), lambda bi, qi: (bi, qi, 0)),
        scratch_shapes=scratch,
        compiler_params=_cparams(("parallel", "parallel")),
        name=name,
    )(qt, k, vt_blocks, *extra_args)


def _attention_gqa(qt, k, vt):
    return _attention(_gqa_kernel, "attn_gqa", qt, k, vt, (), (), A_WIDTH, BF16)


def _attention_diff_out(qt, k, vt, lamv, gsub, lam_init, x, oa, ob, w_out, mods, gate, gpost):
    d = x.shape[-1]
    tq = min(Q_TILE, x.shape[1])
    const = lambda a: pl.BlockSpec(a.shape, lambda bi, qi, n=a.ndim: (0,) * n)
    tok = lambda w: pl.BlockSpec((1, tq, w), lambda bi, qi: (bi, qi, 0))
    args = (lamv, gsub, x, oa, ob, w_out, mods, gpost)
    specs = (const(lamv), const(gsub), tok(d), tok(A_WIDTH), tok(B_WIDTH), const(w_out),
             _mod_spec(mods, gate), const(gpost))
    body = functools.partial(_diff_kernel, lam_init=lam_init)
    return _attention(body, "attn_diff_out", qt, k, vt, args, specs, d, F32)


def _ffn_kernel(xc_ref, xp_ref, xn_ref, shift_ref, scale_ref, gate_ref, gpre_ref, gpost_ref,
                wup_ref, wconv_ref, bconv_ref, wdown_ref, o_ref, act_ref, *, d_ff):
    ti, nt = pl.program_id(1), pl.num_programs(1)
    tm = xc_ref.shape[1]
    rows = tm + 2 * HALO
    xe = jnp.concatenate([xp_ref[0], xc_ref[0], xn_ref[0]], axis=0)
    h = _rms(xe) * gpre_ref[...]
    h = h * (1.0 + scale_ref[0]) + shift_ref[0]
    r = lax.broadcasted_iota(jnp.int32, (rows, 1), 0)
    lo = jnp.where(ti > 0, 0, HALO)
    hi = jnp.where(ti < nt - 1, rows, HALO + tm)
    valid = (r >= lo) & (r < hi)
    hb = jnp.where(valid, h, 0.0).astype(BF16)

    def conv(z, off):
        w = wconv_ref[:, off:off + FF_CHUNK]
        y = (pltpu.roll(z, 1, 0) * w[0:1] + z * w[1:2] + pltpu.roll(z, rows - 1, 0) * w[2:3]
             + bconv_ref[:, off:off + FF_CHUNK])
        return y[HALO:HALO + tm, :]

    for c in range(d_ff // FF_CHUNK):
        og, ov = c * FF_CHUNK, d_ff + c * FF_CHUNK
        zg = jnp.dot(hb, wup_ref[:, og:og + FF_CHUNK], preferred_element_type=F32)
        zv = jnp.dot(hb, wup_ref[:, ov:ov + FF_CHUNK], preferred_element_type=F32)
        act_ref[:, og:og + FF_CHUNK] = (jax.nn.silu(conv(zg, og)) * conv(zv, ov)).astype(BF16)
    f = jnp.dot(act_ref[...], wdown_ref[...], preferred_element_type=F32)
    o_ref[0] = xc_ref[0] + gate_ref[0] * (_rms(f) * gpost_ref[...])


def _conv_ffn(x, mods, shift, scale, gate, gpre, gpost, w_up, w_conv, b_conv, w_down, tm):
    b, t, d = x.shape
    d_ff = w_down.shape[0]
    nt = t // tm
    hb = tm // HALO
    last_hb = t // HALO - 1
    full = lambda a: pl.BlockSpec(a.shape, lambda bi, ti, n=a.ndim: (0,) * n)
    return pl.pallas_call(
        functools.partial(_ffn_kernel, d_ff=d_ff),
        out_shape=jax.ShapeDtypeStruct((b, t, d), F32),
        grid=(b, nt),
        in_specs=[
            pl.BlockSpec((1, tm, d), lambda bi, ti: (bi, ti, 0)),
            pl.BlockSpec((1, HALO, d), lambda bi, ti: (bi, jnp.maximum(ti * hb - 1, 0), 0)),
            pl.BlockSpec((1, HALO, d), lambda bi, ti: (bi, jnp.minimum((ti + 1) * hb, last_hb), 0)),
            _mod_spec(mods, shift), _mod_spec(mods, scale), _mod_spec(mods, gate),
            full(gpre), full(gpost), full(w_up), full(w_conv), full(b_conv), full(w_down),
        ],
        out_specs=pl.BlockSpec((1, tm, d), lambda bi, ti: (bi, ti, 0)),
        scratch_shapes=[pltpu.VMEM((tm, d_ff), BF16)],
        compiler_params=_cparams(("parallel", "parallel")),
        name="conv_ffn",
    )(x, x, x, mods, mods, mods, gpre, gpost, w_up, w_conv, b_conv, w_down)


def _rope_tables(t, dim):
    rows = t // GRID_W
    row = np.repeat(np.arange(rows, dtype=np.float32), GRID_W)
    col = np.tile(np.arange(GRID_W, dtype=np.float32), rows)
    n_ax = dim // 4
    inv = (ROPE_THETA ** (-np.arange(n_ax, dtype=np.float32) / n_ax)).astype(np.float32)
    ang = np.concatenate([row[:, None] * inv, col[:, None] * inv], axis=-1)
    cos, sin = np.cos(ang), np.sin(ang)
    reps = LANES // dim
    return (jnp.asarray(np.tile(np.concatenate([cos, cos], axis=-1), (1, reps)), F32),
            jnp.asarray(np.tile(np.concatenate([-sin, sin], axis=-1), (1, reps)), F32))


def _identity_tables(t):
    return jnp.ones((t, LANES), F32), jnp.zeros((t, LANES), F32)


def kernel(x, c, ctx, c_ctx, w_mod, b_mod, g_pre_mix, g_post_mix, w_in, gq_a, gk_a, gv_b, w_s, b_s,
           lam_q1, lam_k1, lam_q2, lam_k2, g_sub_c, w_out, g_pre_ffn, g_post_ffn, w_up, w_conv,
           b_conv, w_down):
    bsz, t, d = x.shape
    tc = ctx.shape[1]
    depth = w_mod.shape[0]
    assert w_in.shape[-1] == D_IN, "unexpected input-projection width"
    tm_x = min(512, t)
    tm_c = min(512, tc)

    cond = jnp.concatenate([c, c_ctx[None, :], jnp.zeros((SUBLANES - bsz - 1, d), F32)], axis=0)
    mods = _modulation(cond, w_mod, b_mod).reshape(depth * N_MOD * SUBLANES, 1, d)
    mod_x = lambda l, m: ((l * N_MOD + m) * SUBLANES, 1)
    mod_c = lambda l, m: ((l * N_MOD + m) * SUBLANES + bsz, 0)

    cos_a, sin_a = _rope_tables(t, HEAD_DIM)
    cos_c, sin_c = _rope_tables(t, C_QK_DIM)
    one_c, zero_c = _identity_tables(tc)
    lane = np.arange(LANES)
    bd = jnp.asarray(((lane[:, None] // HEAD_DIM) == (lane[None, :] // HEAD_DIM)) / HEAD_DIM, BF16)

    xc = ctx
    for l in range(depth):
        last = l == depth - 1
        lam_init = 0.8 - 0.6 * math.exp(-0.3 * l)
        mx = [mod_x(l, m) for m in range(N_MOD)]
        mc = [mod_c(l, m) for m in range(N_MOD)]
        row = lambda v: v.reshape(1, -1)
        win = w_in[l].astype(BF16)
        gqk = jnp.concatenate([jnp.tile(gq_a[l], A_HEADS), jnp.tile(gk_a[l], A_KV_HEADS)]).reshape(1, -1)
        ws = w_s[l].astype(BF16)
        bsf = jnp.repeat(b_s[l].T, HEAD_DIM, axis=1)
        wout = w_out[l].astype(BF16)
        lamv = jnp.pad(jnp.stack([lam_q1[l], lam_k1[l], lam_q2[l], lam_k2[l]]),
                       ((0, SUBLANES - 4), (0, LANES - C_QK_DIM)))
        gsub = g_sub_c[l].reshape(HEAD_DIM, 1)
        proj_args = (row(g_pre_mix[l]), win, gqk, bd)
        gate_args = (row(gv_b[l]), ws, bsf)

        px = _project(x, mods, mx[0], mx[1], *proj_args, cos_a, sin_a, cos_c, sin_c, *gate_args, tm_x)
        pc = _project(xc, mods, mc[0], mc[1], *proj_args, one_c, zero_c, one_c, zero_c, *gate_args, tm_c)
        qat_x, ka_x, vat_x, ob_x, qct_x, kc_x, vct_x = px
        qat_c, ka_c, vat_c, ob_c, qct_c, kc_c, vct_c = pc

        cat = lambda a_c, a_x, axis: jnp.concatenate([a_c, a_x], axis=axis)
        gpost = row(g_post_mix[l])
        oa = _attention_gqa(qat_x, cat(ka_c, ka_x, 1), cat(vat_c, vat_x, 2))
        x = _attention_diff_out(qct_x, cat(kc_c, kc_x, 1), cat(vct_c, vct_x, 2), lamv, gsub, lam_init,
                                x, oa, ob_x, wout, mods, mx[2], gpost)
        if not last:
            oa_c = _attention_gqa(qat_c, ka_c, vat_c)
            xc = _attention_diff_out(qct_c, kc_c, vct_c, lamv, gsub, lam_init,
                                     xc, oa_c, ob_c, wout, mods, mc[2], gpost)

        ffn_args = (row(g_pre_ffn[l]), row(g_post_ffn[l]), w_up[l].astype(BF16), w_conv[l],
                    row(b_conv[l]), w_down[l].astype(BF16))
        x = _conv_ffn(x, mods, mx[3], mx[4], mx[5], *ffn_args, tm_x)
        if not last:
            xc = _conv_ffn(xc, mods, mc[3], mc[4], mc[5], *ffn_args, tm_c)
    return x
```

```python
import functools
import math

import jax
import jax.numpy as jnp
import numpy as np
from jax import lax
from jax.experimental import pallas as pl
from jax.experimental.pallas import tpu as pltpu

F32 = jnp.float32
BF16 = jnp.bfloat16

GRID_W = 64
HEAD_DIM = 64
A_HEADS = 8
A_KV_HEADS = 2
A_GROUP = A_HEADS // A_KV_HEADS
A_WIDTH = A_HEADS * HEAD_DIM
A_KV_WIDTH = A_KV_HEADS * HEAD_DIM
B_WIDTH = 256
B_GROUPS = B_WIDTH // HEAD_DIM
CHUNK = 128
C_HEADS = 4
C_WIDTH = C_HEADS * HEAD_DIM
C_QK_DIM = HEAD_DIM // 2
CONV_W = 3
ROPE_THETA = 10000.0
EPS = 1e-6
N_MOD = 6
LOG2E = 1.4426950408889634

LANES = 128
SUBLANES = 8
BF16_ROWS = 16
VMEM_LIMIT = 56 * 1024 * 1024
PROJ_ROWS = 256
Q_TILE = 256
Q_TILES_PER_STEP = 2
COL_TILE = 256
KEY_BLOCKS = (768, 512, 256)
SCORE_BUFFERS = 4
NEG_BIG = -1e30
FF_CHUNK = 256
HALO = SUBLANES

_OFF_QA = 0
_OFF_KA = _OFF_QA + A_WIDTH
_OFF_VA = _OFF_KA + A_KV_WIDTH
_OFF_UB = _OFF_VA + A_KV_WIDTH
_OFF_VB = _OFF_UB + B_WIDTH
_OFF_QC = _OFF_VB + B_WIDTH
_OFF_KC = _OFF_QC + C_WIDTH
_OFF_VC = _OFF_KC + C_WIDTH
D_IN = _OFF_VC + C_WIDTH


def _cparams(sem):
    return pltpu.CompilerParams(dimension_semantics=sem, vmem_limit_bytes=VMEM_LIMIT)


def _rms(v):
    return v * lax.rsqrt(jnp.mean(v * v, axis=-1, keepdims=True) + EPS)


def _mod_kernel(c_ref, w_ref, b_ref, o_ref):
    s = jax.nn.silu(c_ref[...])
    o_ref[0, 0] = jnp.dot(s, w_ref[0], preferred_element_type=F32) + b_ref[0, 0]


def _modulation(cond, w_mod, b_mod):
    n_layers, d, _ = w_mod.shape
    rows = cond.shape[0]
    b3 = b_mod.reshape(n_layers, N_MOD, 1, d)
    return pl.pallas_call(
        _mod_kernel,
        out_shape=jax.ShapeDtypeStruct((n_layers, N_MOD, rows, d), F32),
        grid=(n_layers, N_MOD),
        in_specs=[
            pl.BlockSpec((rows, d), lambda l, m: (0, 0)),
            pl.BlockSpec((1, d, d), lambda l, m: (l, 0, m)),
            pl.BlockSpec((1, 1, 1, d), lambda l, m: (l, m, 0, 0)),
        ],
        out_specs=pl.BlockSpec((1, 1, rows, d), lambda l, m: (l, m, 0, 0)),
        compiler_params=_cparams(("arbitrary", "arbitrary")),
        name="modulation",
    )(cond, w_mod, b3)


def _swap_halves(v, half):
    lane = lax.broadcasted_iota(jnp.int32, v.shape, 1)
    first = (lane % (2 * half)) < half
    return jnp.where(first, pltpu.roll(v, LANES - half, 1), pltpu.roll(v, half, 1))


def _proj_kernel(x_ref, shift_ref, scale_ref, gpre_ref, win_ref, gqk_ref, bd_ref,
                 cosa_ref, sina_ref, cosc_ref, sinc_ref, gvb_ref, ws_ref, bsf_ref,
                 qat_ref, ka_ref, vat_ref, ob_ref, qct_ref, kc_ref, vct_ref):
    tm = x_ref.shape[1]
    rg = min(PROJ_ROWS, tm)
    for r0 in range(0, tm, rg):
        _proj_rows(slice(r0, r0 + rg), x_ref, shift_ref, scale_ref, gpre_ref, win_ref, gqk_ref,
                   bd_ref, cosa_ref, sina_ref, cosc_ref, sinc_ref, gvb_ref, ws_ref, bsf_ref,
                   qat_ref, ka_ref, vat_ref, ob_ref, qct_ref, kc_ref, vct_ref)


def _proj_rows(rs, x_ref, shift_ref, scale_ref, gpre_ref, win_ref, gqk_ref, bd_ref,
               cosa_ref, sina_ref, cosc_ref, sinc_ref, gvb_ref, ws_ref, bsf_ref,
               qat_ref, ka_ref, vat_ref, ob_ref, qct_ref, kc_ref, vct_ref):
    tm = rs.stop - rs.start
    h = _rms(x_ref[0, rs, :]) * gpre_ref[...]
    h = h * (1.0 + scale_ref[0]) + shift_ref[0]
    p = jnp.dot(h.astype(BF16), win_ref[0], preferred_element_type=F32)

    cos_a, sin_a = cosa_ref[rs, :], sina_ref[rs, :]
    bd = bd_ref[...]
    a_blocks = []
    for j in range((A_WIDTH + A_KV_WIDTH) // LANES):
        blk = p[:, j * LANES:(j + 1) * LANES]
        sq = blk * blk
        hi = sq.astype(BF16)
        lo = (sq - hi.astype(F32)).astype(BF16)
        msq = (jnp.dot(hi, bd, preferred_element_type=F32)
               + jnp.dot(lo, bd, preferred_element_type=F32))
        nb = blk * lax.rsqrt(msq + EPS) * gqk_ref[:, j * LANES:(j + 1) * LANES]
        a_blocks.append(nb * cos_a + _swap_halves(nb, HEAD_DIM // 2) * sin_a)
    qa = jnp.concatenate(a_blocks[:A_WIDTH // LANES], axis=1) * (HEAD_DIM ** -0.5 * LOG2E)
    qat_ref[0, :, rs] = qa.T.astype(BF16)
    ka_ref[0, rs, :] = a_blocks[A_WIDTH // LANES].astype(BF16)
    vat_ref[0, :, rs] = p[:, _OFF_VA:_OFF_VA + A_KV_WIDTH].T.astype(BF16)

    cos_c, sin_c = cosc_ref[rs, :], sinc_ref[rs, :]

    def rope_c(off):
        blocks = []
        for j in range(C_WIDTH // LANES):
            blk = p[:, off + j * LANES:off + (j + 1) * LANES]
            blocks.append(blk * cos_c + _swap_halves(blk, C_QK_DIM // 2) * sin_c)
        return jnp.concatenate(blocks, axis=1)

    qct_ref[0, :, rs] = (rope_c(_OFF_QC) * (C_QK_DIM ** -0.5 * LOG2E)).T.astype(BF16)
    kc_ref[0, rs, :] = rope_c(_OFF_KC).astype(BF16)
    vct_ref[0, :, rs] = p[:, _OFF_VC:_OFF_VC + C_WIDTH].T.astype(BF16)

    u = jax.nn.gelu(p[:, _OFF_UB:_OFF_UB + B_WIDTH])
    vn = (_rms(jax.nn.gelu(p[:, _OFF_VB:_OFF_VB + B_WIDTH])) * gvb_ref[...]).astype(BF16)
    gid = lax.broadcasted_iota(jnp.int32, (CHUNK, B_WIDTH), 1) // HEAD_DIM
    bias = bsf_ref[...]
    for ci in range(tm // CHUNK):
        vchunk = vn[ci * CHUNK:(ci + 1) * CHUNK, :]
        mixed = jnp.zeros((CHUNK, B_WIDTH), F32)
        for g in range(B_GROUPS):
            r = jnp.dot(ws_ref[g], vchunk, preferred_element_type=F32)
            mixed = jnp.where(gid == g, r, mixed)
        ob_ref[0, rs.start + ci * CHUNK:rs.start + (ci + 1) * CHUNK, :] = (
            u[ci * CHUNK:(ci + 1) * CHUNK, :] * (mixed + bias)).astype(BF16)


def _layer_spec(w, layer):
    return pl.BlockSpec((1,) + w.shape[1:], lambda bi, ti: (layer,) + (0,) * (w.ndim - 1))


def _mod_spec(mods, mod):
    first, step = mod
    return pl.BlockSpec((1, 1, mods.shape[-1]), lambda bi, ti: (first + step * bi, 0, 0))


def _project(x, mods, shift, scale, gpre, win, gqk, bd, cos_a, sin_a, cos_c, sin_c, gvb, ws, bsf, tm):
    b, t, d = x.shape
    nt = t // tm
    full = lambda shape: pl.BlockSpec(shape, lambda bi, ti: (0,) * len(shape))
    tab = pl.BlockSpec((tm, LANES), lambda bi, ti: (ti, 0))
    out_shape = (
        jax.ShapeDtypeStruct((b, A_WIDTH, t), BF16),
        jax.ShapeDtypeStruct((b, t, A_KV_WIDTH), BF16),
        jax.ShapeDtypeStruct((b, A_KV_WIDTH, t), BF16),
        jax.ShapeDtypeStruct((b, t, B_WIDTH), BF16),
        jax.ShapeDtypeStruct((b, C_WIDTH, t), BF16),
        jax.ShapeDtypeStruct((b, t, C_WIDTH), BF16),
        jax.ShapeDtypeStruct((b, C_WIDTH, t), BF16),
    )
    out_specs = (
        pl.BlockSpec((1, A_WIDTH, tm), lambda bi, ti: (bi, 0, ti)),
        pl.BlockSpec((1, tm, A_KV_WIDTH), lambda bi, ti: (bi, ti, 0)),
        pl.BlockSpec((1, A_KV_WIDTH, tm), lambda bi, ti: (bi, 0, ti)),
        pl.BlockSpec((1, tm, B_WIDTH), lambda bi, ti: (bi, ti, 0)),
        pl.BlockSpec((1, C_WIDTH, tm), lambda bi, ti: (bi, 0, ti)),
        pl.BlockSpec((1, tm, C_WIDTH), lambda bi, ti: (bi, ti, 0)),
        pl.BlockSpec((1, C_WIDTH, tm), lambda bi, ti: (bi, 0, ti)),
    )
    return pl.pallas_call(
        _proj_kernel,
        out_shape=out_shape,
        grid=(b, nt),
        in_specs=[
            pl.BlockSpec((1, tm, d), lambda bi, ti: (bi, ti, 0)),
            _mod_spec(mods, shift), _mod_spec(mods, scale),
            full((1, d)),
            _layer_spec(*win),
            full((1, A_WIDTH + A_KV_WIDTH)),
            full((LANES, LANES)),
            tab, tab, tab, tab,
            full((1, B_WIDTH)),
            full((B_GROUPS, CHUNK, CHUNK)),
            full((CHUNK, B_WIDTH)),
        ],
        out_specs=out_specs,
        compiler_params=_cparams(("parallel", "parallel")),
        name="project",
    )(x, mods, mods, gpre, win[0], gqk, bd, cos_a, sin_a, cos_c, sin_c, gvb, ws, bsf)


def _padded_queries(qt, n_blocks, rows_per_block, pad_rows, row_of_block):
    tq = qt.shape[1]
    row = lax.broadcasted_iota(jnp.int32, (pad_rows, tq), 0)
    zero = jnp.zeros((pad_rows, tq), qt.dtype)
    cols = []
    for c in range(n_blocks):
        src = qt[c * rows_per_block:(c + 1) * rows_per_block, :]
        reps = pad_rows // rows_per_block
        tiled = jnp.concatenate([src] * reps, axis=0) if reps > 1 else src
        lo = row_of_block(c)
        keep = (row >= lo) & (row < lo + rows_per_block)
        cols.append(jnp.where(keep, tiled, zero))
    return jnp.concatenate(cols, axis=1)


def _attn_pipeline(qpad_ref, k_ref, vt_ref, s_bufs, n_vgroups):
    nb, tk = vt_ref.shape[1], vt_ref.shape[3]
    n_cols = qpad_ref.shape[1]
    n_ct = n_cols // COL_TILE
    gcols = n_cols // n_vgroups
    ones = jnp.ones((BF16_ROWS, tk), BF16)

    def qk_tile(j, ct, s_ref):
        cs = slice(ct * COL_TILE, (ct + 1) * COL_TILE)
        kb = k_ref[0, pl.ds(pl.multiple_of(j * tk, tk), tk), :]
        s = jnp.dot(kb, qpad_ref[:, cs], preferred_element_type=F32)
        s_ref[ct] = s
        return jnp.max(s, axis=0, keepdims=True)

    def sm_tile(j, ct, s_ref, mblk, m, acc):
        m_new = jnp.maximum(m, mblk)
        alpha = jnp.exp2(m - m_new)
        p = jnp.exp2(s_ref[ct] - m_new).astype(BF16)
        g = (ct * COL_TILE) // gcols
        vext = jnp.concatenate([vt_ref[0, j, g * HEAD_DIM:(g + 1) * HEAD_DIM, :], ones], axis=0)
        return m_new, alpha * acc + jnp.dot(vext, p, preferred_element_type=F32)

    def stage(j_qk, s_qk, j_sm, s_sm, mb_sm, ms, accs):
        mb_new, ms2, accs2 = [], [], []
        for ct in range(n_ct):
            if j_qk is not None:
                mb_new.append(qk_tile(j_qk, ct, s_qk))
            if j_sm is not None:
                m2, a2 = sm_tile(j_sm, ct, s_sm, mb_sm[ct], ms[ct], accs[ct])
                ms2.append(m2)
                accs2.append(a2)
        if j_sm is None:
            ms2, accs2 = ms, accs
        return tuple(mb_new), tuple(ms2), tuple(accs2)

    ms = tuple(jnp.full((1, COL_TILE), NEG_BIG, F32) for _ in range(n_ct))
    accs = tuple(jnp.zeros((HEAD_DIM + BF16_ROWS, COL_TILE), F32) for _ in range(n_ct))
    nbuf = len(s_bufs)
    mb, _, _ = stage(0, s_bufs[0], None, None, None, ms, accs)
    n_steps = nb - 1
    n_iter = n_steps // nbuf
    if n_iter > 0:
        def body(i, carry):
            ms, accs, mb = carry
            for u in range(nbuf):
                j = i * nbuf + u
                mb, ms, accs = stage(j + 1, s_bufs[(u + 1) % nbuf], j, s_bufs[u], mb, ms, accs)
            return ms, accs, mb
        ms, accs, mb = lax.fori_loop(0, n_iter, body, (ms, accs, mb))
    for j in range(n_iter * nbuf, n_steps):
        mb, ms, accs = stage(j + 1, s_bufs[(j + 1) % nbuf], j, s_bufs[j % nbuf], mb, ms, accs)
    _, ms, accs = stage(None, None, nb - 1, s_bufs[(nb - 1) % nbuf], mb, ms, accs)
    return jnp.concatenate(accs, axis=1)


def _query_tiles(qt_ref, qpad_ref):
    tq = qpad_ref.shape[1] // (2 * C_HEADS)
    return tq, [slice(q0, q0 + tq) for q0 in range(0, qt_ref.shape[2], tq)]


def _gqa_kernel(qt_ref, k_ref, vt_ref, o_ref, qpad_ref, *s_bufs):
    tq, tiles = _query_tiles(qt_ref, qpad_ref)
    for qs in tiles:
        _gqa_tile(qs, tq, qt_ref, k_ref, vt_ref, o_ref, qpad_ref, s_bufs)


def _gqa_tile(qs, tq, qt_ref, k_ref, vt_ref, o_ref, qpad_ref, s_bufs):
    qpad_ref[...] = _padded_queries(qt_ref[0, :, qs], A_HEADS, HEAD_DIM, A_KV_WIDTH,
                                    lambda c: (c // A_GROUP) * HEAD_DIM)
    acc = _attn_pipeline(qpad_ref, k_ref, vt_ref, s_bufs, A_KV_HEADS)
    o = acc[0:HEAD_DIM, :] / acc[HEAD_DIM:HEAD_DIM + 1, :]
    ot = jnp.concatenate([o[:, c * tq:(c + 1) * tq] for c in range(A_HEADS)], axis=0)
    o_ref[0, qs, :] = ot.T.astype(o_ref.dtype)


def _diff_kernel(qt_ref, k_ref, vt_ref, lamv_ref, gsub_ref, x_ref, oa_ref, ob_ref, wout_ref,
                 gate_ref, gpost_ref, o_ref, qpad_ref, *s_bufs, lam_init):
    tq, tiles = _query_tiles(qt_ref, qpad_ref)
    for qs in tiles:
        _diff_tile(qs, tq, qt_ref, k_ref, vt_ref, lamv_ref, gsub_ref, x_ref, oa_ref, ob_ref,
                   wout_ref, gate_ref, gpost_ref, o_ref, qpad_ref, s_bufs, lam_init)


def _diff_tile(qs, tq, qt_ref, k_ref, vt_ref, lamv_ref, gsub_ref, x_ref, oa_ref, ob_ref, wout_ref,
               gate_ref, gpost_ref, o_ref, qpad_ref, s_bufs, lam_init):
    qpad_ref[...] = _padded_queries(qt_ref[0, :, qs], 2 * C_HEADS, C_QK_DIM, C_WIDTH,
                                    lambda c: c * C_QK_DIM)
    acc = _attn_pipeline(qpad_ref, k_ref, vt_ref, s_bufs, C_HEADS)
    lv = lamv_ref[...]
    lam = (jnp.exp(jnp.sum(lv[0:1] * lv[1:2], axis=1, keepdims=True))
           - jnp.exp(jnp.sum(lv[2:3] * lv[3:4], axis=1, keepdims=True)) + lam_init)
    a = acc[0:HEAD_DIM, :] / acc[HEAD_DIM:HEAD_DIM + 1, :]
    heads = []
    for hh in range(C_HEADS):
        d = a[:, (2 * hh) * tq:(2 * hh + 1) * tq] - lam * a[:, (2 * hh + 1) * tq:(2 * hh + 2) * tq]
        d = d * lax.rsqrt(jnp.mean(d * d, axis=0, keepdims=True) + EPS) * gsub_ref[...]
        heads.append(d * (1.0 - lam_init))
    oc = jnp.concatenate(heads, axis=0).T.astype(BF16)
    mix = (jnp.dot(oa_ref[0, qs, :], wout_ref[0, 0:A_WIDTH, :], preferred_element_type=F32)
           + jnp.dot(ob_ref[0, qs, :], wout_ref[0, A_WIDTH:A_WIDTH + B_WIDTH, :],
                     preferred_element_type=F32)
           + jnp.dot(oc, wout_ref[0, A_WIDTH + B_WIDTH:, :], preferred_element_type=F32))
    o_ref[0, qs, :] = x_ref[0, qs, :] + gate_ref[0] * (_rms(mix) * gpost_ref[...])


def _key_block(s):
    return next(tk for tk in KEY_BLOCKS if s % tk == 0)


def _queries_per_step(t):
    return min(Q_TILE * Q_TILES_PER_STEP, t)


def _attention(body, name, qt, k, vt, extra_args, extra_specs, out_width, out_dtype):
    b, r, t = qt.shape
    s, wk = k.shape[1:]
    wv = vt.shape[1]
    tq = min(Q_TILE, t)
    tq_step = _queries_per_step(t)
    tk = _key_block(s)
    nb = s // tk
    vt_blocks = jnp.transpose(vt.reshape(b, wv, nb, tk), (0, 2, 1, 3))
    n_cols = 2 * C_HEADS * tq
    once = pl.Buffered(1)
    in_specs = [pl.BlockSpec((1, r, tq_step), lambda bi, qi: (bi, 0, qi)),
                pl.BlockSpec((1, s, wk), lambda bi, qi: (bi, 0, 0), pipeline_mode=once),
                pl.BlockSpec((1, nb, wv, tk), lambda bi, qi: (bi, 0, 0, 0), pipeline_mode=once)]
    scratch = ([pltpu.VMEM((wk, n_cols), BF16)]
               + [pltpu.VMEM((n_cols // COL_TILE, tk, COL_TILE), F32)] * min(SCORE_BUFFERS, nb))
    return pl.pallas_call(
        body,
        out_shape=jax.ShapeDtypeStruct((b, t, out_width), out_dtype),
        grid=(b, t // tq_step),
        in_specs=in_specs + list(extra_specs),
        out_specs=pl.BlockSpec((1, tq_step, out_width), lambda bi, qi: (bi, qi, 0)),
        scratch_shapes=scratch,
        compiler_params=_cparams(("parallel", "parallel")),
        name=name,
    )(qt, k, vt_blocks, *extra_args)


def _attention_gqa(qt, k, vt):
    return _attention(_gqa_kernel, "attn_gqa", qt, k, vt, (), (), A_WIDTH, BF16)


def _attention_diff_out(qt, k, vt, lamv, gsub, lam_init, x, oa, ob, w_out, mods, gate, gpost):
    d = x.shape[-1]
    tq_step = _queries_per_step(x.shape[1])
    const = lambda a: pl.BlockSpec(a.shape, lambda bi, qi, n=a.ndim: (0,) * n)
    tok = lambda w: pl.BlockSpec((1, tq_step, w), lambda bi, qi: (bi, qi, 0))
    args = (lamv, gsub, x, oa, ob, w_out[0], mods, gpost)
    specs = (const(lamv), const(gsub), tok(d), tok(A_WIDTH), tok(B_WIDTH), _layer_spec(*w_out),
             _mod_spec(mods, gate), const(gpost))
    body = functools.partial(_diff_kernel, lam_init=lam_init)
    return _attention(body, "attn_diff_out", qt, k, vt, args, specs, d, F32)


def _ffn_kernel(xc_ref, xp_ref, xn_ref, shift_ref, scale_ref, gate_ref, gpre_ref, gpost_ref,
                wup_ref, wconv_ref, bconv_ref, wdown_ref, o_ref, act_ref, *, d_ff):
    ti, nt = pl.program_id(1), pl.num_programs(1)
    tm = xc_ref.shape[1]
    rows = tm + 2 * HALO
    xe = jnp.concatenate([xp_ref[0], xc_ref[0], xn_ref[0]], axis=0)
    h = _rms(xe) * gpre_ref[...]
    h = h * (1.0 + scale_ref[0]) + shift_ref[0]
    r = lax.broadcasted_iota(jnp.int32, (rows, 1), 0)
    lo = jnp.where(ti > 0, 0, HALO)
    hi = jnp.where(ti < nt - 1, rows, HALO + tm)
    valid = (r >= lo) & (r < hi)
    hb = jnp.where(valid, h, 0.0).astype(BF16)

    def conv(z, off):
        w = wconv_ref[:, off:off + FF_CHUNK]
        y = (pltpu.roll(z, 1, 0) * w[0:1] + z * w[1:2] + pltpu.roll(z, rows - 1, 0) * w[2:3]
             + bconv_ref[:, off:off + FF_CHUNK])
        return y[HALO:HALO + tm, :]

    for c in range(d_ff // FF_CHUNK):
        og, ov = c * FF_CHUNK, d_ff + c * FF_CHUNK
        zg = jnp.dot(hb, wup_ref[0, :, og:og + FF_CHUNK], preferred_element_type=F32)
        zv = jnp.dot(hb, wup_ref[0, :, ov:ov + FF_CHUNK], preferred_element_type=F32)
        act_ref[:, og:og + FF_CHUNK] = (jax.nn.silu(conv(zg, og)) * conv(zv, ov)).astype(BF16)
    f = jnp.dot(act_ref[...], wdown_ref[0], preferred_element_type=F32)
    o_ref[0] = xc_ref[0] + gate_ref[0] * (_rms(f) * gpost_ref[...])


def _conv_ffn(x, mods, shift, scale, gate, gpre, gpost, w_up, w_conv, b_conv, w_down, tm):
    b, t, d = x.shape
    d_ff = w_down[0].shape[1]
    nt = t // tm
    hb = tm // HALO
    last_hb = t // HALO - 1
    full = lambda a: pl.BlockSpec(a.shape, lambda bi, ti, n=a.ndim: (0,) * n)
    return pl.pallas_call(
        functools.partial(_ffn_kernel, d_ff=d_ff),
        out_shape=jax.ShapeDtypeStruct((b, t, d), F32),
        grid=(b, nt),
        in_specs=[
            pl.BlockSpec((1, tm, d), lambda bi, ti: (bi, ti, 0)),
            pl.BlockSpec((1, HALO, d), lambda bi, ti: (bi, jnp.maximum(ti * hb - 1, 0), 0)),
            pl.BlockSpec((1, HALO, d), lambda bi, ti: (bi, jnp.minimum((ti + 1) * hb, last_hb), 0)),
            _mod_spec(mods, shift), _mod_spec(mods, scale), _mod_spec(mods, gate),
            full(gpre), full(gpost), _layer_spec(*w_up), full(w_conv), full(b_conv),
            _layer_spec(*w_down),
        ],
        out_specs=pl.BlockSpec((1, tm, d), lambda bi, ti: (bi, ti, 0)),
        scratch_shapes=[pltpu.VMEM((tm, d_ff), BF16)],
        compiler_params=_cparams(("parallel", "parallel")),
        name="conv_ffn",
    )(x, x, x, mods, mods, mods, gpre, gpost, w_up[0], w_conv, b_conv, w_down[0])


def _rope_tables(t, dim):
    rows = t // GRID_W
    row = np.repeat(np.arange(rows, dtype=np.float32), GRID_W)
    col = np.tile(np.arange(GRID_W, dtype=np.float32), rows)
    n_ax = dim // 4
    inv = (ROPE_THETA ** (-np.arange(n_ax, dtype=np.float32) / n_ax)).astype(np.float32)
    ang = np.concatenate([row[:, None] * inv, col[:, None] * inv], axis=-1)
    cos, sin = np.cos(ang), np.sin(ang)
    reps = LANES // dim
    return (jnp.asarray(np.tile(np.concatenate([cos, cos], axis=-1), (1, reps)), F32),
            jnp.asarray(np.tile(np.concatenate([-sin, sin], axis=-1), (1, reps)), F32))


def _identity_tables(t):
    return jnp.ones((t, LANES), F32), jnp.zeros((t, LANES), F32)


def kernel(x, c, ctx, c_ctx, w_mod, b_mod, g_pre_mix, g_post_mix, w_in, gq_a, gk_a, gv_b, w_s, b_s,
           lam_q1, lam_k1, lam_q2, lam_k2, g_sub_c, w_out, g_pre_ffn, g_post_ffn, w_up, w_conv,
           b_conv, w_down):
    bsz, t, d = x.shape
    tc = ctx.shape[1]
    depth = w_mod.shape[0]
    assert w_in.shape[-1] == D_IN, "unexpected input-projection width"
    tm_x = min(512, t)
    tm_c = min(512, tc)

    cond = jnp.concatenate([c, c_ctx[None, :], jnp.zeros((SUBLANES - bsz - 1, d), F32)], axis=0)
    mods = _modulation(cond, w_mod, b_mod).reshape(depth * N_MOD * SUBLANES, 1, d)
    mod_x = lambda l, m: ((l * N_MOD + m) * SUBLANES, 1)
    mod_c = lambda l, m: ((l * N_MOD + m) * SUBLANES + bsz, 0)

    cos_a, sin_a = _rope_tables(t, HEAD_DIM)
    cos_c, sin_c = _rope_tables(t, C_QK_DIM)
    one_c, zero_c = _identity_tables(tc)
    lane = np.arange(LANES)
    bd = jnp.asarray(((lane[:, None] // HEAD_DIM) == (lane[None, :] // HEAD_DIM)) / HEAD_DIM, BF16)

    w_in_b, w_out_b, w_up_b, w_down_b = (w.astype(BF16) for w in (w_in, w_out, w_up, w_down))

    xc = ctx
    for l in range(depth):
        last = l == depth - 1
        lam_init = 0.8 - 0.6 * math.exp(-0.3 * l)
        mx = [mod_x(l, m) for m in range(N_MOD)]
        mc = [mod_c(l, m) for m in range(N_MOD)]
        row = lambda v: v.reshape(1, -1)
        win = (w_in_b, l)
        gqk = jnp.concatenate([jnp.tile(gq_a[l], A_HEADS), jnp.tile(gk_a[l], A_KV_HEADS)]).reshape(1, -1)
        ws = w_s[l].astype(BF16)
        bsf = jnp.repeat(b_s[l].T, HEAD_DIM, axis=1)
        wout = (w_out_b, l)
        lamv = jnp.pad(jnp.stack([lam_q1[l], lam_k1[l], lam_q2[l], lam_k2[l]]),
                       ((0, SUBLANES - 4), (0, LANES - C_QK_DIM)))
        gsub = g_sub_c[l].reshape(HEAD_DIM, 1)
        proj_args = (row(g_pre_mix[l]), win, gqk, bd)
        gate_args = (row(gv_b[l]), ws, bsf)

        px = _project(x, mods, mx[0], mx[1], *proj_args, cos_a, sin_a, cos_c, sin_c, *gate_args, tm_x)
        pc = _project(xc, mods, mc[0], mc[1], *proj_args, one_c, zero_c, one_c, zero_c, *gate_args, tm_c)
        qat_x, ka_x, vat_x, ob_x, qct_x, kc_x, vct_x = px
        qat_c, ka_c, vat_c, ob_c, qct_c, kc_c, vct_c = pc

        cat = lambda a_c, a_x, axis: jnp.concatenate([a_c, a_x], axis=axis)
        gpost = row(g_post_mix[l])
        oa = _attention_gqa(qat_x, cat(ka_c, ka_x, 1), cat(vat_c, vat_x, 2))
        x = _attention_diff_out(qct_x, cat(kc_c, kc_x, 1), cat(vct_c, vct_x, 2), lamv, gsub, lam_init,
                                x, oa, ob_x, wout, mods, mx[2], gpost)
        if not last:
            oa_c = _attention_gqa(qat_c, ka_c, vat_c)
            xc = _attention_diff_out(qct_c, kc_c, vct_c, lamv, gsub, lam_init,
                                     xc, oa_c, ob_c, wout, mods, mc[2], gpost)

        ffn_args = (row(g_pre_ffn[l]), row(g_post_ffn[l]), (w_up_b, l), w_conv[l],
                    row(b_conv[l]), (w_down_b, l))
        x = _conv_ffn(x, mods, mx[3], mx[4], mx[5], *ffn_args, tm_x)
        if not last:
            xc = _conv_ffn(xc, mods, mc[3], mc[4], mc[5], *ffn_args, tm_c)
    return x
```

```python
import functools
import math

import jax
import jax.numpy as jnp
import numpy as np
from jax import lax
from jax.experimental import pallas as pl
from jax.experimental.pallas import tpu as pltpu

F32 = jnp.float32
BF16 = jnp.bfloat16

GRID_W = 64
HEAD_DIM = 64
A_HEADS = 8
A_KV_HEADS = 2
A_GROUP = A_HEADS // A_KV_HEADS
A_WIDTH = A_HEADS * HEAD_DIM
A_KV_WIDTH = A_KV_HEADS * HEAD_DIM
B_WIDTH = 256
B_GROUPS = B_WIDTH // HEAD_DIM
CHUNK = 128
C_HEADS = 4
C_WIDTH = C_HEADS * HEAD_DIM
C_QK_DIM = HEAD_DIM // 2
CONV_W = 3
ROPE_THETA = 10000.0
EPS = 1e-6
N_MOD = 6
LOG2E = 1.4426950408889634

LANES = 128
SUBLANES = 8
BF16_ROWS = 16
VMEM_LIMIT = 56 * 1024 * 1024
PROJ_ROWS = 256
Q_TILE = 256
Q_TILES_PER_STEP = 2
COL_TILE = 256
KEY_BLOCKS = (768, 512, 256)
SCORE_BUFFERS = 4
NEG_BIG = -1e30
FF_CHUNK = 256
HALO = SUBLANES

_OFF_QA = 0
_OFF_KA = _OFF_QA + A_WIDTH
_OFF_VA = _OFF_KA + A_KV_WIDTH
_OFF_UB = _OFF_VA + A_KV_WIDTH
_OFF_VB = _OFF_UB + B_WIDTH
_OFF_QC = _OFF_VB + B_WIDTH
_OFF_KC = _OFF_QC + C_WIDTH
_OFF_VC = _OFF_KC + C_WIDTH
D_IN = _OFF_VC + C_WIDTH


def _cparams(sem):
    return pltpu.CompilerParams(dimension_semantics=sem, vmem_limit_bytes=VMEM_LIMIT)


def _rms(v):
    return v * lax.rsqrt(jnp.mean(v * v, axis=-1, keepdims=True) + EPS)


def _mod_kernel(c_ref, w_ref, b_ref, o_ref):
    s = jax.nn.silu(c_ref[...])
    o_ref[0, 0] = jnp.dot(s, w_ref[0], preferred_element_type=F32) + b_ref[0, 0]


def _modulation(cond, w_mod, b_mod):
    n_layers, d, _ = w_mod.shape
    rows = cond.shape[0]
    b3 = b_mod.reshape(n_layers, N_MOD, 1, d)
    return pl.pallas_call(
        _mod_kernel,
        out_shape=jax.ShapeDtypeStruct((n_layers, N_MOD, rows, d), F32),
        grid=(n_layers, N_MOD),
        in_specs=[
            pl.BlockSpec((rows, d), lambda l, m: (0, 0)),
            pl.BlockSpec((1, d, d), lambda l, m: (l, 0, m)),
            pl.BlockSpec((1, 1, 1, d), lambda l, m: (l, m, 0, 0)),
        ],
        out_specs=pl.BlockSpec((1, 1, rows, d), lambda l, m: (l, m, 0, 0)),
        compiler_params=_cparams(("arbitrary", "arbitrary")),
        name="modulation",
    )(cond, w_mod, b3)


def _swap_halves(v, half):
    lane = lax.broadcasted_iota(jnp.int32, v.shape, 1)
    first = (lane % (2 * half)) < half
    return jnp.where(first, pltpu.roll(v, LANES - half, 1), pltpu.roll(v, half, 1))


def _proj_kernel(x_ref, shift_ref, scale_ref, gpre_ref, win_ref, gqk_ref, bd_ref,
                 cosa_ref, sina_ref, cosc_ref, sinc_ref, gvb_ref, ws_ref, bsf_ref,
                 qat_ref, ka_ref, vat_ref, ob_ref, qct_ref, kc_ref, vct_ref):
    tm = x_ref.shape[1]
    rg = min(PROJ_ROWS, tm)
    for r0 in range(0, tm, rg):
        _proj_rows(slice(r0, r0 + rg), x_ref, shift_ref, scale_ref, gpre_ref, win_ref, gqk_ref,
                   bd_ref, cosa_ref, sina_ref, cosc_ref, sinc_ref, gvb_ref, ws_ref, bsf_ref,
                   qat_ref, ka_ref, vat_ref, ob_ref, qct_ref, kc_ref, vct_ref)


def _proj_rows(rs, x_ref, shift_ref, scale_ref, gpre_ref, win_ref, gqk_ref, bd_ref,
               cosa_ref, sina_ref, cosc_ref, sinc_ref, gvb_ref, ws_ref, bsf_ref,
               qat_ref, ka_ref, vat_ref, ob_ref, qct_ref, kc_ref, vct_ref):
    tm = rs.stop - rs.start
    h = _rms(x_ref[0, rs, :]) * gpre_ref[...]
    h = h * (1.0 + scale_ref[0]) + shift_ref[0]
    p = jnp.dot(h.astype(BF16), win_ref[0], preferred_element_type=F32)

    cos_a, sin_a = cosa_ref[rs, :], sina_ref[rs, :]
    bd = bd_ref[...]
    a_blocks = []
    for j in range((A_WIDTH + A_KV_WIDTH) // LANES):
        blk = p[:, j * LANES:(j + 1) * LANES]
        sq = blk * blk
        hi = sq.astype(BF16)
        lo = (sq - hi.astype(F32)).astype(BF16)
        msq = (jnp.dot(hi, bd, preferred_element_type=F32)
               + jnp.dot(lo, bd, preferred_element_type=F32))
        nb = blk * lax.rsqrt(msq + EPS) * gqk_ref[:, j * LANES:(j + 1) * LANES]
        a_blocks.append(nb * cos_a + _swap_halves(nb, HEAD_DIM // 2) * sin_a)
    qa = jnp.concatenate(a_blocks[:A_WIDTH // LANES], axis=1) * (HEAD_DIM ** -0.5 * LOG2E)
    qat_ref[0, :, rs] = qa.T.astype(BF16)
    ka_ref[0, rs, :] = a_blocks[A_WIDTH // LANES].astype(BF16)
    vat_ref[0, :, rs] = p[:, _OFF_VA:_OFF_VA + A_KV_WIDTH].T.astype(BF16)

    cos_c, sin_c = cosc_ref[rs, :], sinc_ref[rs, :]

    def rope_c(off):
        blocks = []
        for j in range(C_WIDTH // LANES):
            blk = p[:, off + j * LANES:off + (j + 1) * LANES]
            blocks.append(blk * cos_c + _swap_halves(blk, C_QK_DIM // 2) * sin_c)
        return jnp.concatenate(blocks, axis=1)

    qct_ref[0, :, rs] = (rope_c(_OFF_QC) * (C_QK_DIM ** -0.5 * LOG2E)).T.astype(BF16)
    kc_ref[0, rs, :] = rope_c(_OFF_KC).astype(BF16)
    vct_ref[0, :, rs] = p[:, _OFF_VC:_OFF_VC + C_WIDTH].T.astype(BF16)

    u = jax.nn.gelu(p[:, _OFF_UB:_OFF_UB + B_WIDTH])
    vn = (_rms(jax.nn.gelu(p[:, _OFF_VB:_OFF_VB + B_WIDTH])) * gvb_ref[...]).astype(BF16)
    gid = lax.broadcasted_iota(jnp.int32, (CHUNK, B_WIDTH), 1) // HEAD_DIM
    bias = bsf_ref[...]
    for ci in range(tm // CHUNK):
        vchunk = vn[ci * CHUNK:(ci + 1) * CHUNK, :]
        mixed = jnp.zeros((CHUNK, B_WIDTH), F32)
        for g in range(B_GROUPS):
            r = jnp.dot(ws_ref[g], vchunk, preferred_element_type=F32)
            mixed = jnp.where(gid == g, r, mixed)
        ob_ref[0, rs.start + ci * CHUNK:rs.start + (ci + 1) * CHUNK, :] = (
            u[ci * CHUNK:(ci + 1) * CHUNK, :] * (mixed + bias)).astype(BF16)


def _layer_spec(w, layer):
    return pl.BlockSpec((1,) + w.shape[1:], lambda bi, ti: (layer,) + (0,) * (w.ndim - 1))


def _mod_spec(mods, mod):
    first, step = mod
    return pl.BlockSpec((1, 1, mods.shape[-1]), lambda bi, ti: (first + step * bi, 0, 0))


def _project(x, mods, shift, scale, gpre, win, gqk, bd, cos_a, sin_a, cos_c, sin_c, gvb, ws, bsf, tm):
    b, t, d = x.shape
    nt = t // tm
    full = lambda shape: pl.BlockSpec(shape, lambda bi, ti: (0,) * len(shape))
    tab = pl.BlockSpec((tm, LANES), lambda bi, ti: (ti, 0))
    out_shape = (
        jax.ShapeDtypeStruct((b, A_WIDTH, t), BF16),
        jax.ShapeDtypeStruct((b, t, A_KV_WIDTH), BF16),
        jax.ShapeDtypeStruct((b, A_KV_WIDTH, t), BF16),
        jax.ShapeDtypeStruct((b, t, B_WIDTH), BF16),
        jax.ShapeDtypeStruct((b, C_WIDTH, t), BF16),
        jax.ShapeDtypeStruct((b, t, C_WIDTH), BF16),
        jax.ShapeDtypeStruct((b, C_WIDTH, t), BF16),
    )
    out_specs = (
        pl.BlockSpec((1, A_WIDTH, tm), lambda bi, ti: (bi, 0, ti)),
        pl.BlockSpec((1, tm, A_KV_WIDTH), lambda bi, ti: (bi, ti, 0)),
        pl.BlockSpec((1, A_KV_WIDTH, tm), lambda bi, ti: (bi, 0, ti)),
        pl.BlockSpec((1, tm, B_WIDTH), lambda bi, ti: (bi, ti, 0)),
        pl.BlockSpec((1, C_WIDTH, tm), lambda bi, ti: (bi, 0, ti)),
        pl.BlockSpec((1, tm, C_WIDTH), lambda bi, ti: (bi, ti, 0)),
        pl.BlockSpec((1, C_WIDTH, tm), lambda bi, ti: (bi, 0, ti)),
    )
    return pl.pallas_call(
        _proj_kernel,
        out_shape=out_shape,
        grid=(b, nt),
        in_specs=[
            pl.BlockSpec((1, tm, d), lambda bi, ti: (bi, ti, 0)),
            _mod_spec(mods, shift), _mod_spec(mods, scale),
            full((1, d)),
            _layer_spec(*win),
            full((1, A_WIDTH + A_KV_WIDTH)),
            full((LANES, LANES)),
            tab, tab, tab, tab,
            full((1, B_WIDTH)),
            full((B_GROUPS, CHUNK, CHUNK)),
            full((CHUNK, B_WIDTH)),
        ],
        out_specs=out_specs,
        compiler_params=_cparams(("parallel", "parallel")),
        name="project",
    )(x, mods, mods, gpre, win[0], gqk, bd, cos_a, sin_a, cos_c, sin_c, gvb, ws, bsf)


def _padded_queries(qt, n_blocks, rows_per_block, pad_rows, row_of_block):
    tq = qt.shape[1]
    row = lax.broadcasted_iota(jnp.int32, (pad_rows, tq), 0)
    zero = jnp.zeros((pad_rows, tq), qt.dtype)
    cols = []
    for c in range(n_blocks):
        src = qt[c * rows_per_block:(c + 1) * rows_per_block, :]
        reps = pad_rows // rows_per_block
        tiled = jnp.concatenate([src] * reps, axis=0) if reps > 1 else src
        lo = row_of_block(c)
        keep = (row >= lo) & (row < lo + rows_per_block)
        cols.append(jnp.where(keep, tiled, zero))
    return jnp.concatenate(cols, axis=1)


def _attn_pipeline(qpad_ref, k_ref, vt_ref, s_bufs, n_vgroups):
    tk = s_bufs[0].shape[1]
    nb = vt_ref.shape[2] // tk
    n_cols = qpad_ref.shape[1]
    n_ct = n_cols // COL_TILE
    gcols = n_cols // n_vgroups
    ones = jnp.ones((BF16_ROWS, tk), BF16)

    def qk_tile(j, ct, s_ref):
        cs = slice(ct * COL_TILE, (ct + 1) * COL_TILE)
        kb = k_ref[0, pl.ds(pl.multiple_of(j * tk, tk), tk), :]
        s = jnp.dot(kb, qpad_ref[:, cs], preferred_element_type=F32)
        s_ref[ct] = s
        return jnp.max(s, axis=0, keepdims=True)

    def sm_tile(j, ct, s_ref, mblk, m, acc):
        m_new = jnp.maximum(m, mblk)
        alpha = jnp.exp2(m - m_new)
        p = jnp.exp2(s_ref[ct] - m_new).astype(BF16)
        g = (ct * COL_TILE) // gcols
        vblk = vt_ref[0, g * HEAD_DIM:(g + 1) * HEAD_DIM, pl.ds(pl.multiple_of(j * tk, tk), tk)]
        vext = jnp.concatenate([vblk, ones], axis=0)
        return m_new, alpha * acc + jnp.dot(vext, p, preferred_element_type=F32)

    def stage(j_qk, s_qk, j_sm, s_sm, mb_sm, ms, accs):
        mb_new, ms2, accs2 = [], [], []
        for ct in range(n_ct):
            if j_qk is not None:
                mb_new.append(qk_tile(j_qk, ct, s_qk))
            if j_sm is not None:
                m2, a2 = sm_tile(j_sm, ct, s_sm, mb_sm[ct], ms[ct], accs[ct])
                ms2.append(m2)
                accs2.append(a2)
        if j_sm is None:
            ms2, accs2 = ms, accs
        return tuple(mb_new), tuple(ms2), tuple(accs2)

    ms = tuple(jnp.full((1, COL_TILE), NEG_BIG, F32) for _ in range(n_ct))
    accs = tuple(jnp.zeros((HEAD_DIM + BF16_ROWS, COL_TILE), F32) for _ in range(n_ct))
    nbuf = len(s_bufs)
    mb, _, _ = stage(0, s_bufs[0], None, None, None, ms, accs)
    n_steps = nb - 1
    n_iter = n_steps // nbuf
    if n_iter > 0:
        def body(i, carry):
            ms, accs, mb = carry
            for u in range(nbuf):
                j = i * nbuf + u
                mb, ms, accs = stage(j + 1, s_bufs[(u + 1) % nbuf], j, s_bufs[u], mb, ms, accs)
            return ms, accs, mb
        ms, accs, mb = lax.fori_loop(0, n_iter, body, (ms, accs, mb))
    for j in range(n_iter * nbuf, n_steps):
        mb, ms, accs = stage(j + 1, s_bufs[(j + 1) % nbuf], j, s_bufs[j % nbuf], mb, ms, accs)
    _, ms, accs = stage(None, None, nb - 1, s_bufs[(nb - 1) % nbuf], mb, ms, accs)
    return jnp.concatenate(accs, axis=1)


def _query_tiles(qt_ref, qpad_ref):
    tq = qpad_ref.shape[1] // (2 * C_HEADS)
    return tq, [slice(q0, q0 + tq) for q0 in range(0, qt_ref.shape[2], tq)]


def _gqa_kernel(qt_ref, k_ref, vt_ref, o_ref, qpad_ref, *s_bufs):
    tq, tiles = _query_tiles(qt_ref, qpad_ref)
    for qs in tiles:
        _gqa_tile(qs, tq, qt_ref, k_ref, vt_ref, o_ref, qpad_ref, s_bufs)


def _gqa_tile(qs, tq, qt_ref, k_ref, vt_ref, o_ref, qpad_ref, s_bufs):
    qpad_ref[...] = _padded_queries(qt_ref[0, :, qs], A_HEADS, HEAD_DIM, A_KV_WIDTH,
                                    lambda c: (c // A_GROUP) * HEAD_DIM)
    acc = _attn_pipeline(qpad_ref, k_ref, vt_ref, s_bufs, A_KV_HEADS)
    o = acc[0:HEAD_DIM, :] / acc[HEAD_DIM:HEAD_DIM + 1, :]
    ot = jnp.concatenate([o[:, c * tq:(c + 1) * tq] for c in range(A_HEADS)], axis=0)
    o_ref[0, qs, :] = ot.T.astype(o_ref.dtype)


def _diff_kernel(qt_ref, k_ref, vt_ref, lamv_ref, gsub_ref, x_ref, oa_ref, ob_ref, wout_ref,
                 gate_ref, gpost_ref, o_ref, qpad_ref, *s_bufs, lam_init):
    tq, tiles = _query_tiles(qt_ref, qpad_ref)
    for qs in tiles:
        _diff_tile(qs, tq, qt_ref, k_ref, vt_ref, lamv_ref, gsub_ref, x_ref, oa_ref, ob_ref,
                   wout_ref, gate_ref, gpost_ref, o_ref, qpad_ref, s_bufs, lam_init)


def _diff_tile(qs, tq, qt_ref, k_ref, vt_ref, lamv_ref, gsub_ref, x_ref, oa_ref, ob_ref, wout_ref,
               gate_ref, gpost_ref, o_ref, qpad_ref, s_bufs, lam_init):
    qpad_ref[...] = _padded_queries(qt_ref[0, :, qs], 2 * C_HEADS, C_QK_DIM, C_WIDTH,
                                    lambda c: c * C_QK_DIM)
    acc = _attn_pipeline(qpad_ref, k_ref, vt_ref, s_bufs, C_HEADS)
    lv = lamv_ref[...]
    lam = (jnp.exp(jnp.sum(lv[0:1] * lv[1:2], axis=1, keepdims=True))
           - jnp.exp(jnp.sum(lv[2:3] * lv[3:4], axis=1, keepdims=True)) + lam_init)
    a = acc[0:HEAD_DIM, :] / acc[HEAD_DIM:HEAD_DIM + 1, :]
    heads = []
    for hh in range(C_HEADS):
        d = a[:, (2 * hh) * tq:(2 * hh + 1) * tq] - lam * a[:, (2 * hh + 1) * tq:(2 * hh + 2) * tq]
        d = d * lax.rsqrt(jnp.mean(d * d, axis=0, keepdims=True) + EPS) * gsub_ref[...]
        heads.append(d * (1.0 - lam_init))
    oc = jnp.concatenate(heads, axis=0).T.astype(BF16)
    mix = (jnp.dot(oa_ref[0, qs, :], wout_ref[0, 0:A_WIDTH, :], preferred_element_type=F32)
           + jnp.dot(ob_ref[0, qs, :], wout_ref[0, A_WIDTH:A_WIDTH + B_WIDTH, :],
                     preferred_element_type=F32)
           + jnp.dot(oc, wout_ref[0, A_WIDTH + B_WIDTH:, :], preferred_element_type=F32))
    o_ref[0, qs, :] = x_ref[0, qs, :] + gate_ref[0] * (_rms(mix) * gpost_ref[...])


def _key_block(s):
    return next(tk for tk in KEY_BLOCKS if s % tk == 0)


def _queries_per_step(t):
    return min(Q_TILE * Q_TILES_PER_STEP, t)


def _attention(body, name, qt, k, vt, extra_args, extra_specs, out_width, out_dtype):
    b, r, t = qt.shape
    s, wk = k.shape[1:]
    wv = vt.shape[1]
    tq = min(Q_TILE, t)
    tq_step = _queries_per_step(t)
    tk = _key_block(s)
    nb = s // tk
    n_cols = 2 * C_HEADS * tq
    once = pl.Buffered(1)
    in_specs = [pl.BlockSpec((1, r, tq_step), lambda bi, qi: (bi, 0, qi)),
                pl.BlockSpec((1, s, wk), lambda bi, qi: (bi, 0, 0), pipeline_mode=once),
                pl.BlockSpec((1, wv, s), lambda bi, qi: (bi, 0, 0), pipeline_mode=once)]
    scratch = ([pltpu.VMEM((wk, n_cols), BF16)]
               + [pltpu.VMEM((n_cols // COL_TILE, tk, COL_TILE), F32)] * min(SCORE_BUFFERS, nb))
    return pl.pallas_call(
        body,
        out_shape=jax.ShapeDtypeStruct((b, t, out_width), out_dtype),
        grid=(b, t // tq_step),
        in_specs=in_specs + list(extra_specs),
        out_specs=pl.BlockSpec((1, tq_step, out_width), lambda bi, qi: (bi, qi, 0)),
        scratch_shapes=scratch,
        compiler_params=_cparams(("parallel", "parallel")),
        name=name,
    )(qt, k, vt, *extra_args)


def _attention_gqa(qt, k, vt):
    return _attention(_gqa_kernel, "attn_gqa", qt, k, vt, (), (), A_WIDTH, BF16)


def _attention_diff_out(qt, k, vt, lamv, gsub, lam_init, x, oa, ob, w_out, mods, gate, gpost):
    d = x.shape[-1]
    tq_step = _queries_per_step(x.shape[1])
    const = lambda a: pl.BlockSpec(a.shape, lambda bi, qi, n=a.ndim: (0,) * n)
    tok = lambda w: pl.BlockSpec((1, tq_step, w), lambda bi, qi: (bi, qi, 0))
    args = (lamv, gsub, x, oa, ob, w_out[0], mods, gpost)
    specs = (const(lamv), const(gsub), tok(d), tok(A_WIDTH), tok(B_WIDTH), _layer_spec(*w_out),
             _mod_spec(mods, gate), const(gpost))
    body = functools.partial(_diff_kernel, lam_init=lam_init)
    return _attention(body, "attn_diff_out", qt, k, vt, args, specs, d, F32)


def _ffn_kernel(xc_ref, xp_ref, xn_ref, shift_ref, scale_ref, gate_ref, gpre_ref, gpost_ref,
                wup_ref, wconv_ref, bconv_ref, wdown_ref, o_ref, act_ref, *, d_ff):
    ti, nt = pl.program_id(1), pl.num_programs(1)
    tm = xc_ref.shape[1]
    rows = tm + 2 * HALO
    xe = jnp.concatenate([xp_ref[0], xc_ref[0], xn_ref[0]], axis=0)
    h = _rms(xe) * gpre_ref[...]
    h = h * (1.0 + scale_ref[0]) + shift_ref[0]
    r = lax.broadcasted_iota(jnp.int32, (rows, 1), 0)
    lo = jnp.where(ti > 0, 0, HALO)
    hi = jnp.where(ti < nt - 1, rows, HALO + tm)
    valid = (r >= lo) & (r < hi)
    hb = jnp.where(valid, h, 0.0).astype(BF16)

    def conv(z, off):
        w = wconv_ref[:, off:off + FF_CHUNK]
        y = (pltpu.roll(z, 1, 0) * w[0:1] + z * w[1:2] + pltpu.roll(z, rows - 1, 0) * w[2:3]
             + bconv_ref[:, off:off + FF_CHUNK])
        return y[HALO:HALO + tm, :]

    for c in range(d_ff // FF_CHUNK):
        og, ov = c * FF_CHUNK, d_ff + c * FF_CHUNK
        zg = jnp.dot(hb, wup_ref[0, :, og:og + FF_CHUNK], preferred_element_type=F32)
        zv = jnp.dot(hb, wup_ref[0, :, ov:ov + FF_CHUNK], preferred_element_type=F32)
        act_ref[:, og:og + FF_CHUNK] = (jax.nn.silu(conv(zg, og)) * conv(zv, ov)).astype(BF16)
    f = jnp.dot(act_ref[...], wdown_ref[0], preferred_element_type=F32)
    o_ref[0] = xc_ref[0] + gate_ref[0] * (_rms(f) * gpost_ref[...])


def _conv_ffn(x, mods, shift, scale, gate, gpre, gpost, w_up, w_conv, b_conv, w_down, tm):
    b, t, d = x.shape
    d_ff = w_down[0].shape[1]
    nt = t // tm
    hb = tm // HALO
    last_hb = t // HALO - 1
    full = lambda a: pl.BlockSpec(a.shape, lambda bi, ti, n=a.ndim: (0,) * n)
    return pl.pallas_call(
        functools.partial(_ffn_kernel, d_ff=d_ff),
        out_shape=jax.ShapeDtypeStruct((b, t, d), F32),
        grid=(b, nt),
        in_specs=[
            pl.BlockSpec((1, tm, d), lambda bi, ti: (bi, ti, 0)),
            pl.BlockSpec((1, HALO, d), lambda bi, ti: (bi, jnp.maximum(ti * hb - 1, 0), 0)),
            pl.BlockSpec((1, HALO, d), lambda bi, ti: (bi, jnp.minimum((ti + 1) * hb, last_hb), 0)),
            _mod_spec(mods, shift), _mod_spec(mods, scale), _mod_spec(mods, gate),
            full(gpre), full(gpost), _layer_spec(*w_up), full(w_conv), full(b_conv),
            _layer_spec(*w_down),
        ],
        out_specs=pl.BlockSpec((1, tm, d), lambda bi, ti: (bi, ti, 0)),
        scratch_shapes=[pltpu.VMEM((tm, d_ff), BF16)],
        compiler_params=_cparams(("parallel", "parallel")),
        name="conv_ffn",
    )(x, x, x, mods, mods, mods, gpre, gpost, w_up[0], w_conv, b_conv, w_down[0])


def _rope_tables(t, dim):
    rows = t // GRID_W
    row = np.repeat(np.arange(rows, dtype=np.float32), GRID_W)
    col = np.tile(np.arange(GRID_W, dtype=np.float32), rows)
    n_ax = dim // 4
    inv = (ROPE_THETA ** (-np.arange(n_ax, dtype=np.float32) / n_ax)).astype(np.float32)
    ang = np.concatenate([row[:, None] * inv, col[:, None] * inv], axis=-1)
    cos, sin = np.cos(ang), np.sin(ang)
    reps = LANES // dim
    return (jnp.asarray(np.tile(np.concatenate([cos, cos], axis=-1), (1, reps)), F32),
            jnp.asarray(np.tile(np.concatenate([-sin, sin], axis=-1), (1, reps)), F32))


def _identity_tables(t):
    return jnp.ones((t, LANES), F32), jnp.zeros((t, LANES), F32)


def kernel(x, c, ctx, c_ctx, w_mod, b_mod, g_pre_mix, g_post_mix, w_in, gq_a, gk_a, gv_b, w_s, b_s,
           lam_q1, lam_k1, lam_q2, lam_k2, g_sub_c, w_out, g_pre_ffn, g_post_ffn, w_up, w_conv,
           b_conv, w_down):
    bsz, t, d = x.shape
    tc = ctx.shape[1]
    depth = w_mod.shape[0]
    assert w_in.shape[-1] == D_IN, "unexpected input-projection width"
    tm_x = min(512, t)
    tm_c = min(512, tc)

    cond = jnp.concatenate([c, c_ctx[None, :], jnp.zeros((SUBLANES - bsz - 1, d), F32)], axis=0)
    mods = _modulation(cond, w_mod, b_mod).reshape(depth * N_MOD * SUBLANES, 1, d)
    mod_x = lambda l, m: ((l * N_MOD + m) * SUBLANES, 1)
    mod_c = lambda l, m: ((l * N_MOD + m) * SUBLANES + bsz, 0)

    cos_a, sin_a = _rope_tables(t, HEAD_DIM)
    cos_c, sin_c = _rope_tables(t, C_QK_DIM)
    one_c, zero_c = _identity_tables(tc)
    lane = np.arange(LANES)
    bd = jnp.asarray(((lane[:, None] // HEAD_DIM) == (lane[None, :] // HEAD_DIM)) / HEAD_DIM, BF16)

    w_in_b, w_out_b, w_up_b, w_down_b = (w.astype(BF16) for w in (w_in, w_out, w_up, w_down))

    xc = ctx
    for l in range(depth):
        last = l == depth - 1
        lam_init = 0.8 - 0.6 * math.exp(-0.3 * l)
        mx = [mod_x(l, m) for m in range(N_MOD)]
        mc = [mod_c(l, m) for m in range(N_MOD)]
        row = lambda v: v.reshape(1, -1)
        win = (w_in_b, l)
        gqk = jnp.concatenate([jnp.tile(gq_a[l], A_HEADS), jnp.tile(gk_a[l], A_KV_HEADS)]).reshape(1, -1)
        ws = w_s[l].astype(BF16)
        bsf = jnp.repeat(b_s[l].T, HEAD_DIM, axis=1)
        wout = (w_out_b, l)
        lamv = jnp.pad(jnp.stack([lam_q1[l], lam_k1[l], lam_q2[l], lam_k2[l]]),
                       ((0, SUBLANES - 4), (0, LANES - C_QK_DIM)))
        gsub = g_sub_c[l].reshape(HEAD_DIM, 1)
        proj_args = (row(g_pre_mix[l]), win, gqk, bd)
        gate_args = (row(gv_b[l]), ws, bsf)

        px = _project(x, mods, mx[0], mx[1], *proj_args, cos_a, sin_a, cos_c, sin_c, *gate_args, tm_x)
        pc = _project(xc, mods, mc[0], mc[1], *proj_args, one_c, zero_c, one_c, zero_c, *gate_args, tm_c)
        qat_x, ka_x, vat_x, ob_x, qct_x, kc_x, vct_x = px
        qat_c, ka_c, vat_c, ob_c, qct_c, kc_c, vct_c = pc

        cat = lambda a_c, a_x, axis: jnp.concatenate([a_c, a_x], axis=axis)
        gpost = row(g_post_mix[l])
        oa = _attention_gqa(qat_x, cat(ka_c, ka_x, 1), cat(vat_c, vat_x, 2))
        x = _attention_diff_out(qct_x, cat(kc_c, kc_x, 1), cat(vct_c, vct_x, 2), lamv, gsub, lam_init,
                                x, oa, ob_x, wout, mods, mx[2], gpost)
        if not last:
            oa_c = _attention_gqa(qat_c, ka_c, vat_c)
            xc = _attention_diff_out(qct_c, kc_c, vct_c, lamv, gsub, lam_init,
                                     xc, oa_c, ob_c, wout, mods, mc[2], gpost)

        ffn_args = (row(g_pre_ffn[l]), row(g_post_ffn[l]), (w_up_b, l), w_conv[l],
                    row(b_conv[l]), (w_down_b, l))
        x = _conv_ffn(x, mods, mx[3], mx[4], mx[5], *ffn_args, tm_x)
        if not last:
            xc = _conv_ffn(xc, mods, mc[3], mc[4], mc[5], *ffn_args, tm_c)
    return x
```

```python
import functools
import math

import jax
import jax.numpy as jnp
import numpy as np
from jax import lax
from jax.experimental import pallas as pl
from jax.experimental.pallas import tpu as pltpu

F32 = jnp.float32
BF16 = jnp.bfloat16

GRID_W = 64
HEAD_DIM = 64
A_HEADS = 8
A_KV_HEADS = 2
A_GROUP = A_HEADS // A_KV_HEADS
A_WIDTH = A_HEADS * HEAD_DIM
A_KV_WIDTH = A_KV_HEADS * HEAD_DIM
B_WIDTH = 256
B_GROUPS = B_WIDTH // HEAD_DIM
CHUNK = 128
C_HEADS = 4
C_WIDTH = C_HEADS * HEAD_DIM
C_QK_DIM = HEAD_DIM // 2
CONV_W = 3
ROPE_THETA = 10000.0
EPS = 1e-6
N_MOD = 6
LOG2E = 1.4426950408889634

LANES = 128
SUBLANES = 8
BF16_ROWS = 16
VMEM_LIMIT = 56 * 1024 * 1024
PROJ_TILE = 1024
PROJ_ROWS = 256
FFN_TILE = 512
Q_TILE = 256
Q_TILES_PER_STEP = 2
COL_TILE = 256
KEY_BLOCKS = (768, 512, 256)
SCORE_BUFFERS = 4
NEG_BIG = -1e30
FF_CHUNK = 256
HALO = SUBLANES

_OFF_QA = 0
_OFF_KA = _OFF_QA + A_WIDTH
_OFF_VA = _OFF_KA + A_KV_WIDTH
_OFF_UB = _OFF_VA + A_KV_WIDTH
_OFF_VB = _OFF_UB + B_WIDTH
_OFF_QC = _OFF_VB + B_WIDTH
_OFF_KC = _OFF_QC + C_WIDTH
_OFF_VC = _OFF_KC + C_WIDTH
D_IN = _OFF_VC + C_WIDTH


def _cparams(sem):
    return pltpu.CompilerParams(dimension_semantics=sem, vmem_limit_bytes=VMEM_LIMIT)


def _rms(v):
    return v * lax.rsqrt(jnp.mean(v * v, axis=-1, keepdims=True) + EPS)


def _mod_kernel(c_ref, w_ref, b_ref, o_ref):
    s = jax.nn.silu(c_ref[...])
    o_ref[0, 0] = jnp.dot(s, w_ref[0], preferred_element_type=F32) + b_ref[0, 0]


def _modulation(cond, w_mod, b_mod):
    n_layers, d, _ = w_mod.shape
    rows = cond.shape[0]
    b3 = b_mod.reshape(n_layers, N_MOD, 1, d)
    return pl.pallas_call(
        _mod_kernel,
        out_shape=jax.ShapeDtypeStruct((n_layers, N_MOD, rows, d), F32),
        grid=(n_layers, N_MOD),
        in_specs=[
            pl.BlockSpec((rows, d), lambda l, m: (0, 0)),
            pl.BlockSpec((1, d, d), lambda l, m: (l, 0, m)),
            pl.BlockSpec((1, 1, 1, d), lambda l, m: (l, m, 0, 0)),
        ],
        out_specs=pl.BlockSpec((1, 1, rows, d), lambda l, m: (l, m, 0, 0)),
        compiler_params=_cparams(("arbitrary", "arbitrary")),
        name="modulation",
    )(cond, w_mod, b3)


def _swap_halves(v, half):
    lane = lax.broadcasted_iota(jnp.int32, v.shape, 1)
    first = (lane % (2 * half)) < half
    return jnp.where(first, pltpu.roll(v, LANES - half, 1), pltpu.roll(v, half, 1))


def _proj_kernel(x_ref, shift_ref, scale_ref, gpre_ref, win_ref, gqk_ref, bd_ref,
                 cosa_ref, sina_ref, cosc_ref, sinc_ref, gvb_ref, ws_ref, bsf_ref,
                 qat_ref, ka_ref, vat_ref, ob_ref, qct_ref, kc_ref, vct_ref):
    tm = x_ref.shape[1]
    rg = min(PROJ_ROWS, tm)
    for r0 in range(0, tm, rg):
        _proj_rows(slice(r0, r0 + rg), x_ref, shift_ref, scale_ref, gpre_ref, win_ref, gqk_ref,
                   bd_ref, cosa_ref, sina_ref, cosc_ref, sinc_ref, gvb_ref, ws_ref, bsf_ref,
                   qat_ref, ka_ref, vat_ref, ob_ref, qct_ref, kc_ref, vct_ref)


def _proj_rows(rs, x_ref, shift_ref, scale_ref, gpre_ref, win_ref, gqk_ref, bd_ref,
               cosa_ref, sina_ref, cosc_ref, sinc_ref, gvb_ref, ws_ref, bsf_ref,
               qat_ref, ka_ref, vat_ref, ob_ref, qct_ref, kc_ref, vct_ref):
    tm = rs.stop - rs.start
    h = _rms(x_ref[0, rs, :]) * gpre_ref[...]
    h = h * (1.0 + scale_ref[0]) + shift_ref[0]
    p = jnp.dot(h.astype(BF16), win_ref[0], preferred_element_type=F32)

    cos_a, sin_a = cosa_ref[rs, :], sina_ref[rs, :]
    bd = bd_ref[...]
    a_blocks = []
    for j in range((A_WIDTH + A_KV_WIDTH) // LANES):
        blk = p[:, j * LANES:(j + 1) * LANES]
        sq = blk * blk
        hi = sq.astype(BF16)
        lo = (sq - hi.astype(F32)).astype(BF16)
        msq = (jnp.dot(hi, bd, preferred_element_type=F32)
               + jnp.dot(lo, bd, preferred_element_type=F32))
        nb = blk * lax.rsqrt(msq + EPS) * gqk_ref[:, j * LANES:(j + 1) * LANES]
        a_blocks.append(nb * cos_a + _swap_halves(nb, HEAD_DIM // 2) * sin_a)
    qa = jnp.concatenate(a_blocks[:A_WIDTH // LANES], axis=1) * (HEAD_DIM ** -0.5 * LOG2E)
    qat_ref[0, :, rs] = qa.T.astype(BF16)
    ka_ref[0, rs, :] = a_blocks[A_WIDTH // LANES].astype(BF16)
    vat_ref[0, :, rs] = p[:, _OFF_VA:_OFF_VA + A_KV_WIDTH].T.astype(BF16)

    cos_c, sin_c = cosc_ref[rs, :], sinc_ref[rs, :]

    def rope_c(off):
        blocks = []
        for j in range(C_WIDTH // LANES):
            blk = p[:, off + j * LANES:off + (j + 1) * LANES]
            blocks.append(blk * cos_c + _swap_halves(blk, C_QK_DIM // 2) * sin_c)
        return jnp.concatenate(blocks, axis=1)

    qct_ref[0, :, rs] = (rope_c(_OFF_QC) * (C_QK_DIM ** -0.5 * LOG2E)).T.astype(BF16)
    kc_ref[0, rs, :] = rope_c(_OFF_KC).astype(BF16)
    vct_ref[0, :, rs] = p[:, _OFF_VC:_OFF_VC + C_WIDTH].T.astype(BF16)

    u = jax.nn.gelu(p[:, _OFF_UB:_OFF_UB + B_WIDTH])
    vn = (_rms(jax.nn.gelu(p[:, _OFF_VB:_OFF_VB + B_WIDTH])) * gvb_ref[...]).astype(BF16)
    gid = lax.broadcasted_iota(jnp.int32, (CHUNK, B_WIDTH), 1) // HEAD_DIM
    bias = bsf_ref[...]
    for ci in range(tm // CHUNK):
        vchunk = vn[ci * CHUNK:(ci + 1) * CHUNK, :]
        mixed = jnp.zeros((CHUNK, B_WIDTH), F32)
        for g in range(B_GROUPS):
            r = jnp.dot(ws_ref[g], vchunk, preferred_element_type=F32)
            mixed = jnp.where(gid == g, r, mixed)
        ob_ref[0, rs.start + ci * CHUNK:rs.start + (ci + 1) * CHUNK, :] = (
            u[ci * CHUNK:(ci + 1) * CHUNK, :] * (mixed + bias)).astype(BF16)


def _layer_spec(w, layer):
    return pl.BlockSpec((1,) + w.shape[1:], lambda bi, ti: (layer,) + (0,) * (w.ndim - 1))


def _mod_spec(mods, mod):
    first, step = mod
    return pl.BlockSpec((1, 1, mods.shape[-1]), lambda bi, ti: (first + step * bi, 0, 0))


def _project(x, mods, shift, scale, gpre, win, gqk, bd, cos_a, sin_a, cos_c, sin_c, gvb, ws, bsf, tm):
    b, t, d = x.shape
    nt = t // tm
    full = lambda shape: pl.BlockSpec(shape, lambda bi, ti: (0,) * len(shape))
    tab = pl.BlockSpec((tm, LANES), lambda bi, ti: (ti, 0))
    out_shape = (
        jax.ShapeDtypeStruct((b, A_WIDTH, t), BF16),
        jax.ShapeDtypeStruct((b, t, A_KV_WIDTH), BF16),
        jax.ShapeDtypeStruct((b, A_KV_WIDTH, t), BF16),
        jax.ShapeDtypeStruct((b, t, B_WIDTH), BF16),
        jax.ShapeDtypeStruct((b, C_WIDTH, t), BF16),
        jax.ShapeDtypeStruct((b, t, C_WIDTH), BF16),
        jax.ShapeDtypeStruct((b, C_WIDTH, t), BF16),
    )
    out_specs = (
        pl.BlockSpec((1, A_WIDTH, tm), lambda bi, ti: (bi, 0, ti)),
        pl.BlockSpec((1, tm, A_KV_WIDTH), lambda bi, ti: (bi, ti, 0)),
        pl.BlockSpec((1, A_KV_WIDTH, tm), lambda bi, ti: (bi, 0, ti)),
        pl.BlockSpec((1, tm, B_WIDTH), lambda bi, ti: (bi, ti, 0)),
        pl.BlockSpec((1, C_WIDTH, tm), lambda bi, ti: (bi, 0, ti)),
        pl.BlockSpec((1, tm, C_WIDTH), lambda bi, ti: (bi, ti, 0)),
        pl.BlockSpec((1, C_WIDTH, tm), lambda bi, ti: (bi, 0, ti)),
    )
    return pl.pallas_call(
        _proj_kernel,
        out_shape=out_shape,
        grid=(b, nt),
        in_specs=[
            pl.BlockSpec((1, tm, d), lambda bi, ti: (bi, ti, 0)),
            _mod_spec(mods, shift), _mod_spec(mods, scale),
            full((1, d)),
            _layer_spec(*win),
            full((1, A_WIDTH + A_KV_WIDTH)),
            full((LANES, LANES)),
            tab, tab, tab, tab,
            full((1, B_WIDTH)),
            full((B_GROUPS, CHUNK, CHUNK)),
            full((CHUNK, B_WIDTH)),
        ],
        out_specs=out_specs,
        compiler_params=_cparams(("parallel", "parallel")),
        name="project",
    )(x, mods, mods, gpre, win[0], gqk, bd, cos_a, sin_a, cos_c, sin_c, gvb, ws, bsf)


def _padded_queries(qt, n_blocks, rows_per_block, pad_rows, row_of_block):
    tq = qt.shape[1]
    row = lax.broadcasted_iota(jnp.int32, (pad_rows, tq), 0)
    zero = jnp.zeros((pad_rows, tq), qt.dtype)
    cols = []
    for c in range(n_blocks):
        src = qt[c * rows_per_block:(c + 1) * rows_per_block, :]
        reps = pad_rows // rows_per_block
        tiled = jnp.concatenate([src] * reps, axis=0) if reps > 1 else src
        lo = row_of_block(c)
        keep = (row >= lo) & (row < lo + rows_per_block)
        cols.append(jnp.where(keep, tiled, zero))
    return jnp.concatenate(cols, axis=1)


def _attn_pipeline(qpad_ref, k_ref, vt_ref, s_bufs, n_vgroups):
    nb, tk = vt_ref.shape[1], vt_ref.shape[3]
    n_cols = qpad_ref.shape[1]
    n_ct = n_cols // COL_TILE
    gcols = n_cols // n_vgroups
    ones = jnp.ones((BF16_ROWS, tk), BF16)

    def qk_tile(j, ct, s_ref):
        cs = slice(ct * COL_TILE, (ct + 1) * COL_TILE)
        kb = k_ref[0, pl.ds(pl.multiple_of(j * tk, tk), tk), :]
        s = jnp.dot(kb, qpad_ref[:, cs], preferred_element_type=F32)
        s_ref[ct] = s
        return jnp.max(s, axis=0, keepdims=True)

    def sm_tile(j, ct, s_ref, mblk, m, acc):
        m_new = jnp.maximum(m, mblk)
        alpha = jnp.exp2(m - m_new)
        p = jnp.exp2(s_ref[ct] - m_new).astype(BF16)
        g = (ct * COL_TILE) // gcols
        vext = jnp.concatenate([vt_ref[0, j, g * HEAD_DIM:(g + 1) * HEAD_DIM, :], ones], axis=0)
        return m_new, alpha * acc + jnp.dot(vext, p, preferred_element_type=F32)

    def stage(j_qk, s_qk, j_sm, s_sm, mb_sm, ms, accs):
        mb_new, ms2, accs2 = [], [], []
        for ct in range(n_ct):
            if j_qk is not None:
                mb_new.append(qk_tile(j_qk, ct, s_qk))
            if j_sm is not None:
                m2, a2 = sm_tile(j_sm, ct, s_sm, mb_sm[ct], ms[ct], accs[ct])
                ms2.append(m2)
                accs2.append(a2)
        if j_sm is None:
            ms2, accs2 = ms, accs
        return tuple(mb_new), tuple(ms2), tuple(accs2)

    ms = tuple(jnp.full((1, COL_TILE), NEG_BIG, F32) for _ in range(n_ct))
    accs = tuple(jnp.zeros((HEAD_DIM + BF16_ROWS, COL_TILE), F32) for _ in range(n_ct))
    nbuf = len(s_bufs)
    mb, _, _ = stage(0, s_bufs[0], None, None, None, ms, accs)
    n_steps = nb - 1
    n_iter = n_steps // nbuf
    if n_iter > 0:
        def body(i, carry):
            ms, accs, mb = carry
            for u in range(nbuf):
                j = i * nbuf + u
                mb, ms, accs = stage(j + 1, s_bufs[(u + 1) % nbuf], j, s_bufs[u], mb, ms, accs)
            return ms, accs, mb
        ms, accs, mb = lax.fori_loop(0, n_iter, body, (ms, accs, mb))
    for j in range(n_iter * nbuf, n_steps):
        mb, ms, accs = stage(j + 1, s_bufs[(j + 1) % nbuf], j, s_bufs[j % nbuf], mb, ms, accs)
    _, ms, accs = stage(None, None, nb - 1, s_bufs[(nb - 1) % nbuf], mb, ms, accs)
    return jnp.concatenate(accs, axis=1)


def _query_tiles(qt_ref, qpad_ref):
    tq = qpad_ref.shape[1] // (2 * C_HEADS)
    return tq, [slice(q0, q0 + tq) for q0 in range(0, qt_ref.shape[2], tq)]


def _gqa_kernel(qt_ref, k_ref, vt_ref, o_ref, qpad_ref, *s_bufs):
    tq, tiles = _query_tiles(qt_ref, qpad_ref)
    for qs in tiles:
        _gqa_tile(qs, tq, qt_ref, k_ref, vt_ref, o_ref, qpad_ref, s_bufs)


def _gqa_tile(qs, tq, qt_ref, k_ref, vt_ref, o_ref, qpad_ref, s_bufs):
    qpad_ref[...] = _padded_queries(qt_ref[0, :, qs], A_HEADS, HEAD_DIM, A_KV_WIDTH,
                                    lambda c: (c // A_GROUP) * HEAD_DIM)
    acc = _attn_pipeline(qpad_ref, k_ref, vt_ref, s_bufs, A_KV_HEADS)
    o = acc[0:HEAD_DIM, :] / acc[HEAD_DIM:HEAD_DIM + 1, :]
    ot = jnp.concatenate([o[:, c * tq:(c + 1) * tq] for c in range(A_HEADS)], axis=0)
    o_ref[0, qs, :] = ot.T.astype(o_ref.dtype)


def _diff_kernel(qt_ref, k_ref, vt_ref, lamv_ref, gsub_ref, x_ref, oa_ref, ob_ref, wout_ref,
                 gate_ref, gpost_ref, o_ref, qpad_ref, *s_bufs, lam_init):
    tq, tiles = _query_tiles(qt_ref, qpad_ref)
    for qs in tiles:
        _diff_tile(qs, tq, qt_ref, k_ref, vt_ref, lamv_ref, gsub_ref, x_ref, oa_ref, ob_ref,
                   wout_ref, gate_ref, gpost_ref, o_ref, qpad_ref, s_bufs, lam_init)


def _diff_tile(qs, tq, qt_ref, k_ref, vt_ref, lamv_ref, gsub_ref, x_ref, oa_ref, ob_ref, wout_ref,
               gate_ref, gpost_ref, o_ref, qpad_ref, s_bufs, lam_init):
    qpad_ref[...] = _padded_queries(qt_ref[0, :, qs], 2 * C_HEADS, C_QK_DIM, C_WIDTH,
                                    lambda c: c * C_QK_DIM)
    acc = _attn_pipeline(qpad_ref, k_ref, vt_ref, s_bufs, C_HEADS)
    lv = lamv_ref[...]
    lam = (jnp.exp(jnp.sum(lv[0:1] * lv[1:2], axis=1, keepdims=True))
           - jnp.exp(jnp.sum(lv[2:3] * lv[3:4], axis=1, keepdims=True)) + lam_init)
    a = acc[0:HEAD_DIM, :] / acc[HEAD_DIM:HEAD_DIM + 1, :]
    heads = []
    for hh in range(C_HEADS):
        d = a[:, (2 * hh) * tq:(2 * hh + 1) * tq] - lam * a[:, (2 * hh + 1) * tq:(2 * hh + 2) * tq]
        d = d * lax.rsqrt(jnp.mean(d * d, axis=0, keepdims=True) + EPS) * gsub_ref[...]
        heads.append(d * (1.0 - lam_init))
    oc = jnp.concatenate(heads, axis=0).T.astype(BF16)
    mix = (jnp.dot(oa_ref[0, qs, :], wout_ref[0, 0:A_WIDTH, :], preferred_element_type=F32)
           + jnp.dot(ob_ref[0, qs, :], wout_ref[0, A_WIDTH:A_WIDTH + B_WIDTH, :],
                     preferred_element_type=F32)
           + jnp.dot(oc, wout_ref[0, A_WIDTH + B_WIDTH:, :], preferred_element_type=F32))
    o_ref[0, qs, :] = x_ref[0, qs, :] + gate_ref[0] * (_rms(mix) * gpost_ref[...])


def _key_block(s):
    return next(tk for tk in KEY_BLOCKS if s % tk == 0)


def _queries_per_step(t):
    return min(Q_TILE * Q_TILES_PER_STEP, t)


def _attention(body, name, qt, k, vt, extra_args, extra_specs, out_width, out_dtype):
    b, r, t = qt.shape
    s, wk = k.shape[1:]
    wv = vt.shape[1]
    tq = min(Q_TILE, t)
    tq_step = _queries_per_step(t)
    tk = _key_block(s)
    nb = s // tk
    vt_blocks = jnp.transpose(vt.reshape(b, wv, nb, tk), (0, 2, 1, 3))
    n_cols = 2 * C_HEADS * tq
    once = pl.Buffered(1)
    in_specs = [pl.BlockSpec((1, r, tq_step), lambda bi, qi: (bi, 0, qi)),
                pl.BlockSpec((1, s, wk), lambda bi, qi: (bi, 0, 0), pipeline_mode=once),
                pl.BlockSpec((1, nb, wv, tk), lambda bi, qi: (bi, 0, 0, 0), pipeline_mode=once)]
    scratch = ([pltpu.VMEM((wk, n_cols), BF16)]
               + [pltpu.VMEM((n_cols // COL_TILE, tk, COL_TILE), F32)] * min(SCORE_BUFFERS, nb))
    return pl.pallas_call(
        body,
        out_shape=jax.ShapeDtypeStruct((b, t, out_width), out_dtype),
        grid=(b, t // tq_step),
        in_specs=in_specs + list(extra_specs),
        out_specs=pl.BlockSpec((1, tq_step, out_width), lambda bi, qi: (bi, qi, 0)),
        scratch_shapes=scratch,
        compiler_params=_cparams(("parallel", "parallel")),
        name=name,
    )(qt, k, vt_blocks, *extra_args)


def _attention_gqa(qt, k, vt):
    return _attention(_gqa_kernel, "attn_gqa", qt, k, vt, (), (), A_WIDTH, BF16)


def _attention_diff_out(qt, k, vt, lamv, gsub, lam_init, x, oa, ob, w_out, mods, gate, gpost):
    d = x.shape[-1]
    tq_step = _queries_per_step(x.shape[1])
    const = lambda a: pl.BlockSpec(a.shape, lambda bi, qi, n=a.ndim: (0,) * n)
    tok = lambda w: pl.BlockSpec((1, tq_step, w), lambda bi, qi: (bi, qi, 0))
    args = (lamv, gsub, x, oa, ob, w_out[0], mods, gpost)
    specs = (const(lamv), const(gsub), tok(d), tok(A_WIDTH), tok(B_WIDTH), _layer_spec(*w_out),
             _mod_spec(mods, gate), const(gpost))
    body = functools.partial(_diff_kernel, lam_init=lam_init)
    return _attention(body, "attn_diff_out", qt, k, vt, args, specs, d, F32)


def _ffn_kernel(xc_ref, xp_ref, xn_ref, shift_ref, scale_ref, gate_ref, gpre_ref, gpost_ref,
                wup_ref, wconv_ref, bconv_ref, wdown_ref, o_ref, act_ref, *, d_ff):
    ti, nt = pl.program_id(1), pl.num_programs(1)
    tm = xc_ref.shape[1]
    rows = tm + 2 * HALO
    xe = jnp.concatenate([xp_ref[0], xc_ref[0], xn_ref[0]], axis=0)
    h = _rms(xe) * gpre_ref[...]
    h = h * (1.0 + scale_ref[0]) + shift_ref[0]
    r = lax.broadcasted_iota(jnp.int32, (rows, 1), 0)
    lo = jnp.where(ti > 0, 0, HALO)
    hi = jnp.where(ti < nt - 1, rows, HALO + tm)
    valid = (r >= lo) & (r < hi)
    hb = jnp.where(valid, h, 0.0).astype(BF16)

    def conv(z, off):
        w = wconv_ref[:, off:off + FF_CHUNK]
        y = (pltpu.roll(z, 1, 0) * w[0:1] + z * w[1:2] + pltpu.roll(z, rows - 1, 0) * w[2:3]
             + bconv_ref[:, off:off + FF_CHUNK])
        return y[HALO:HALO + tm, :]

    for c in range(d_ff // FF_CHUNK):
        og, ov = c * FF_CHUNK, d_ff + c * FF_CHUNK
        zg = jnp.dot(hb, wup_ref[0, :, og:og + FF_CHUNK], preferred_element_type=F32)
        zv = jnp.dot(hb, wup_ref[0, :, ov:ov + FF_CHUNK], preferred_element_type=F32)
        act_ref[:, og:og + FF_CHUNK] = (jax.nn.silu(conv(zg, og)) * conv(zv, ov)).astype(BF16)
    f = jnp.dot(act_ref[...], wdown_ref[0], preferred_element_type=F32)
    o_ref[0] = xc_ref[0] + gate_ref[0] * (_rms(f) * gpost_ref[...])


def _conv_ffn(x, mods, shift, scale, gate, gpre, gpost, w_up, w_conv, b_conv, w_down, tm):
    b, t, d = x.shape
    d_ff = w_down[0].shape[1]
    nt = t // tm
    hb = tm // HALO
    last_hb = t // HALO - 1
    full = lambda a: pl.BlockSpec(a.shape, lambda bi, ti, n=a.ndim: (0,) * n)
    return pl.pallas_call(
        functools.partial(_ffn_kernel, d_ff=d_ff),
        out_shape=jax.ShapeDtypeStruct((b, t, d), F32),
        grid=(b, nt),
        in_specs=[
            pl.BlockSpec((1, tm, d), lambda bi, ti: (bi, ti, 0)),
            pl.BlockSpec((1, HALO, d), lambda bi, ti: (bi, jnp.maximum(ti * hb - 1, 0), 0)),
            pl.BlockSpec((1, HALO, d), lambda bi, ti: (bi, jnp.minimum((ti + 1) * hb, last_hb), 0)),
            _mod_spec(mods, shift), _mod_spec(mods, scale), _mod_spec(mods, gate),
            full(gpre), full(gpost), _layer_spec(*w_up), full(w_conv), full(b_conv),
            _layer_spec(*w_down),
        ],
        out_specs=pl.BlockSpec((1, tm, d), lambda bi, ti: (bi, ti, 0)),
        scratch_shapes=[pltpu.VMEM((tm, d_ff), BF16)],
        compiler_params=_cparams(("parallel", "parallel")),
        name="conv_ffn",
    )(x, x, x, mods, mods, mods, gpre, gpost, w_up[0], w_conv, b_conv, w_down[0])


def _rope_tables(t, dim):
    rows = t // GRID_W
    row = np.repeat(np.arange(rows, dtype=np.float32), GRID_W)
    col = np.tile(np.arange(GRID_W, dtype=np.float32), rows)
    n_ax = dim // 4
    inv = (ROPE_THETA ** (-np.arange(n_ax, dtype=np.float32) / n_ax)).astype(np.float32)
    ang = np.concatenate([row[:, None] * inv, col[:, None] * inv], axis=-1)
    cos, sin = np.cos(ang), np.sin(ang)
    reps = LANES // dim
    return (jnp.asarray(np.tile(np.concatenate([cos, cos], axis=-1), (1, reps)), F32),
            jnp.asarray(np.tile(np.concatenate([-sin, sin], axis=-1), (1, reps)), F32))


def _identity_tables(t):
    return jnp.ones((t, LANES), F32), jnp.zeros((t, LANES), F32)


def kernel(x, c, ctx, c_ctx, w_mod, b_mod, g_pre_mix, g_post_mix, w_in, gq_a, gk_a, gv_b, w_s, b_s,
           lam_q1, lam_k1, lam_q2, lam_k2, g_sub_c, w_out, g_pre_ffn, g_post_ffn, w_up, w_conv,
           b_conv, w_down):
    bsz, t, d = x.shape
    tc = ctx.shape[1]
    depth = w_mod.shape[0]
    assert w_in.shape[-1] == D_IN, "unexpected input-projection width"
    tm_x = min(FFN_TILE, t)
    tm_c = min(FFN_TILE, tc)
    tp_x = min(PROJ_TILE, t)
    tp_c = min(PROJ_TILE, tc)

    cond = jnp.concatenate([c, c_ctx[None, :], jnp.zeros((SUBLANES - bsz - 1, d), F32)], axis=0)
    mods = _modulation(cond, w_mod, b_mod).reshape(depth * N_MOD * SUBLANES, 1, d)
    mod_x = lambda l, m: ((l * N_MOD + m) * SUBLANES, 1)
    mod_c = lambda l, m: ((l * N_MOD + m) * SUBLANES + bsz, 0)

    cos_a, sin_a = _rope_tables(t, HEAD_DIM)
    cos_c, sin_c = _rope_tables(t, C_QK_DIM)
    one_c, zero_c = _identity_tables(tc)
    lane = np.arange(LANES)
    bd = jnp.asarray(((lane[:, None] // HEAD_DIM) == (lane[None, :] // HEAD_DIM)) / HEAD_DIM, BF16)

    w_in_b, w_out_b, w_up_b, w_down_b = (w.astype(BF16) for w in (w_in, w_out, w_up, w_down))

    xc = ctx
    for l in range(depth):
        last = l == depth - 1
        lam_init = 0.8 - 0.6 * math.exp(-0.3 * l)
        mx = [mod_x(l, m) for m in range(N_MOD)]
        mc = [mod_c(l, m) for m in range(N_MOD)]
        row = lambda v: v.reshape(1, -1)
        win = (w_in_b, l)
        gqk = jnp.concatenate([jnp.tile(gq_a[l], A_HEADS), jnp.tile(gk_a[l], A_KV_HEADS)]).reshape(1, -1)
        ws = w_s[l].astype(BF16)
        bsf = jnp.repeat(b_s[l].T, HEAD_DIM, axis=1)
        wout = (w_out_b, l)
        lamv = jnp.pad(jnp.stack([lam_q1[l], lam_k1[l], lam_q2[l], lam_k2[l]]),
                       ((0, SUBLANES - 4), (0, LANES - C_QK_DIM)))
        gsub = g_sub_c[l].reshape(HEAD_DIM, 1)
        proj_args = (row(g_pre_mix[l]), win, gqk, bd)
        gate_args = (row(gv_b[l]), ws, bsf)

        px = _project(x, mods, mx[0], mx[1], *proj_args, cos_a, sin_a, cos_c, sin_c, *gate_args, tp_x)
        pc = _project(xc, mods, mc[0], mc[1], *proj_args, one_c, zero_c, one_c, zero_c, *gate_args, tp_c)
        qat_x, ka_x, vat_x, ob_x, qct_x, kc_x, vct_x = px
        qat_c, ka_c, vat_c, ob_c, qct_c, kc_c, vct_c = pc

        cat = lambda a_c, a_x, axis: jnp.concatenate([a_c, a_x], axis=axis)
        gpost = row(g_post_mix[l])
        oa = _attention_gqa(qat_x, cat(ka_c, ka_x, 1), cat(vat_c, vat_x, 2))
        x = _attention_diff_out(qct_x, cat(kc_c, kc_x, 1), cat(vct_c, vct_x, 2), lamv, gsub, lam_init,
                                x, oa, ob_x, wout, mods, mx[2], gpost)
        if not last:
            oa_c = _attention_gqa(qat_c, ka_c, vat_c)
            xc = _attention_diff_out(qct_c, kc_c, vct_c, lamv, gsub, lam_init,
                                     xc, oa_c, ob_c, wout, mods, mc[2], gpost)

        ffn_args = (row(g_pre_ffn[l]), row(g_post_ffn[l]), (w_up_b, l), w_conv[l],
                    row(b_conv[l]), (w_down_b, l))
        x = _conv_ffn(x, mods, mx[3], mx[4], mx[5], *ffn_args, tm_x)
        if not last:
            xc = _conv_ffn(xc, mods, mc[3], mc[4], mc[5], *ffn_args, tm_c)
    return x
```
